```python
import math
import jax, jax.numpy as jnp
from jax import lax
import numpy as np

D_MODEL = 1024
BATCH = 8
SEQ = 4096
DEPTH = 4

HEAD_DIM = 64
D_MIX = 2 * D_MODEL
A_HEADS = D_MIX // 4 // HEAD_DIM
A_WIDTH = A_HEADS * HEAD_DIM
A_PATTERNS = ((128, 1), (512, 4), (2048, 16))
B_WIDTH = D_MIX // 2
B_HEADS = B_WIDTH // HEAD_DIM
B_GROUPS = 2
B_STATE = 128
B_CONV = 4
B_CHUNK = 128
C_HEADS = D_MIX // 4 // HEAD_DIM
C_KV_HEADS = 2
C_REP = C_HEADS // C_KV_HEADS
C_WIDTH = C_HEADS * HEAD_DIM
C_KV_WIDTH = C_KV_HEADS * HEAD_DIM
C_WINDOW = 128
BAND_BLOCK = 128
NORM_EPS = 1e-6

A_COLS = 4 * A_WIDTH
B_CONV_CH = B_WIDTH + 2 * B_GROUPS * B_STATE
B_COLS = B_WIDTH + B_CONV_CH + B_HEADS
C_COLS = 2 * C_WIDTH + 2 * C_KV_WIDTH
IN_COLS = A_COLS + B_COLS + C_COLS

kernel_name = "hymba_dilated_ssd_sinkswa_trunk"


def rms_norm(x, w):
    xf = x.astype(jnp.float32)
    y = xf * lax.rsqrt(jnp.mean(xf * xf, axis=-1, keepdims=True) + NORM_EPS)
    return (y * w.astype(jnp.float32)).astype(x.dtype)


def banded_window_attention(q, k, v, window, sinks=None):
    n, L, hkv, r, dh = q.shape
    blk = BAND_BLOCK
    nb = -(-L // blk)
    pad = nb * blk - L
    if pad:
        q = jnp.pad(q, ((0, 0), (0, pad), (0, 0), (0, 0), (0, 0)))
        k = jnp.pad(k, ((0, 0), (0, pad), (0, 0), (0, 0)))
        v = jnp.pad(v, ((0, 0), (0, pad), (0, 0), (0, 0)))
    qb = q.reshape(n, nb, blk, hkv, r, dh)
    kb = k.reshape(n, nb, blk, hkv, dh)
    vb = v.reshape(n, nb, blk, hkv, dh)
    zero = jnp.zeros_like(kb[:, :1])
    kk = jnp.concatenate([jnp.concatenate([zero, kb[:, :-1]], axis=1), kb], axis=2)
    vv = jnp.concatenate([jnp.concatenate([zero, vb[:, :-1]], axis=1), vb], axis=2)
    s = jnp.einsum('nbqhrd,nbkhd->nbhrqk', qb, kk,
                   preferred_element_type=jnp.float32) * (dh ** -0.5)
    qi = jnp.arange(blk)[:, None]
    kj = jnp.arange(2 * blk)[None, :]
    dist = blk + qi - kj
    band = (dist >= 0) & (dist <= window)
    has_prev = (jnp.arange(nb)[:, None, None] > 0) | (kj >= blk)[None]
    valid = band[None] & has_prev
    s = jnp.where(valid[None, :, None, None], s, -jnp.inf)
    m = jnp.max(s, axis=-1)
    if sinks is not None:
        sink = sinks.astype(jnp.float32)[None, None, :, :, None]
        m = jnp.maximum(m, sink)
    p = jnp.exp(s - m[..., None])
    l = jnp.sum(p, axis=-1)
    if sinks is not None:
        l = l + jnp.exp(sink - m)
    o = jnp.einsum('nbhrqk,nbkhd->nbqhrd', p, vv.astype(jnp.float32))
    m = jnp.moveaxis(m, -1, 2)
    l = jnp.moveaxis(l, -1, 2)
    o = o / l[..., None]
    o = o.reshape(n, nb * blk, hkv, r, dh)[:, :L]
    m = m.reshape(n, nb * blk, hkv, r)[:, :L]
    l = l.reshape(n, nb * blk, hkv, r)[:, :L]
    return o, m, l


def to_strided(t, dil):
    b, s = t.shape[0], t.shape[1]
    t = t.reshape(b, s // dil, dil, *t.shape[2:])
    t = jnp.moveaxis(t, 2, 1)
    return t.reshape(b * dil, s // dil, *t.shape[3:])


def from_strided(t, b, dil):
    L = t.shape[1]
    t = t.reshape(b, dil, L, *t.shape[2:])
    t = jnp.moveaxis(t, 1, 2)
    return t.reshape(b, L * dil, *t.shape[3:])


def dilated_attention(q, k, v):
    b = q.shape[0]
    outs, ms, ls = [], [], []
    for window, dil in A_PATTERNS:
        o, m, l = banded_window_attention(to_strided(q, dil)[:, :, :, None], to_strided(k, dil),
                                          to_strided(v, dil), window // dil)
        outs.append(from_strided(o[:, :, :, 0], b, dil))
        ms.append(from_strided(m[:, :, :, 0], b, dil))
        ls.append(from_strided(l[:, :, :, 0], b, dil))
    m_all = jnp.stack(ms)
    l_all = jnp.stack(ls)
    o_all = jnp.stack(outs)
    wts = l_all * jnp.exp(m_all - jnp.max(m_all, axis=0))
    wts = wts / jnp.sum(wts, axis=0)
    return jnp.sum(o_all * wts[..., None], axis=0)


def causal_depthwise_conv(x, w, bias):
    ch = x.shape[-1]
    y = lax.conv_general_dilated(x, w[:, None, :].astype(x.dtype), window_strides=(1,),
                                 padding=((w.shape[0] - 1, 0),),
                                 dimension_numbers=('NWC', 'WIO', 'NWC'),
                                 feature_group_count=ch)
    return y + bias.astype(x.dtype)


def ssd_scan(xs, dt, a, bm, cm):
    b, s, h, p = xs.shape
    g, n = bm.shape[2], bm.shape[3]
    hg = h // g
    q = B_CHUNK
    nc = s // q
    f32 = jnp.float32
    x = xs.astype(f32).reshape(b, nc, q, g, hg, p)
    dtc = dt.reshape(b, nc, q, g, hg)
    bc = bm.astype(f32).reshape(b, nc, q, g, n)
    cc = cm.astype(f32).reshape(b, nc, q, g, n)
    xdt = x * dtc[..., None]
    a_cum = jnp.cumsum(dtc * a.reshape(g, hg), axis=2)
    seg = a_cum[:, :, :, None] - a_cum[:, :, None, :]
    causal = jnp.tril(jnp.ones((q, q), dtype=bool))
    decay = jnp.exp(jnp.where(causal[:, :, None, None], seg, -jnp.inf))
    cb = jnp.einsum('bclgn,bcsgn->bclsg', cc, bc)
    y_diag = jnp.einsum('bclsgh,bcsghp->bclghp', cb[..., None] * decay, xdt)
    decay_to_end = jnp.exp(a_cum[:, :, -1:] - a_cum)
    states = jnp.einsum('bclgn,bclghp->bcghpn', bc, xdt * decay_to_end[..., None])
    chunk_decay = jnp.exp(a_cum[:, :, -1])

    def step(h_state, inp):
        st, dec = inp
        return h_state * dec[..., None, None] + st, h_state

    h0 = jnp.zeros((b, g, hg, p, n), f32)
    _, h_in = lax.scan(step, h0, (jnp.moveaxis(states, 1, 0), jnp.moveaxis(chunk_decay, 1, 0)))
    h_in = jnp.moveaxis(h_in, 0, 1)
    y_off = jnp.einsum('bclgn,bcghpn->bclghp', cc, h_in) * jnp.exp(a_cum)[..., None]
    return (y_diag + y_off).reshape(b, s, h, p)


def mamba2_mixer(z, xbc, dt_raw, conv_w, conv_b, dt_bias, a_log, d_skip, norm_w):
    b, s, _ = xbc.shape
    xbc = jax.nn.silu(causal_depthwise_conv(xbc, conv_w, conv_b))
    xs, bm, cm = jnp.split(xbc, [B_WIDTH, B_WIDTH + B_GROUPS * B_STATE], axis=-1)
    xs = xs.reshape(b, s, B_HEADS, HEAD_DIM)
    bm = bm.reshape(b, s, B_GROUPS, B_STATE)
    cm = cm.reshape(b, s, B_GROUPS, B_STATE)
    dt = jax.nn.softplus(dt_raw.astype(jnp.float32) + dt_bias.astype(jnp.float32))
    a = -jnp.exp(a_log.astype(jnp.float32))
    y = ssd_scan(xs, dt, a, bm, cm)
    y = y + d_skip.astype(jnp.float32)[:, None] * xs.astype(jnp.float32)
    y = y.reshape(b, s, B_WIDTH) * jax.nn.silu(z.astype(jnp.float32))
    yg = y.reshape(b, s, B_GROUPS, B_WIDTH // B_GROUPS)
    yg = yg * lax.rsqrt(jnp.mean(yg * yg, axis=-1, keepdims=True) + NORM_EPS)
    y = yg.reshape(b, s, B_WIDTH) * norm_w.astype(jnp.float32)
    return y.astype(z.dtype)


def setup_inputs(seed: int = 0) -> dict:
    key = jax.random.key(seed)
    ks = jax.random.split(key, 16)
    f32 = jnp.float32
    x = jax.random.normal(ks[0], (BATCH, SEQ, D_MODEL), f32)
    c = jax.random.normal(ks[1], (BATCH, D_MODEL), f32)
    ada_w = jax.random.normal(ks[2], (DEPTH, D_MODEL, 3 * D_MODEL), f32) * (0.5 * D_MODEL ** -0.5)
    ada_b = jax.random.normal(ks[3], (DEPTH, 3 * D_MODEL), f32) * 0.02
    pre_norm_w = 1.0 + 0.02 * jax.random.normal(ks[4], (DEPTH, D_MODEL), f32)
    post_norm_w = 1.0 + 0.02 * jax.random.normal(ks[5], (DEPTH, D_MODEL), f32)
    w_in = jax.random.normal(ks[6], (DEPTH, D_MODEL, IN_COLS), f32) * D_MODEL ** -0.5
    conv_w = jax.random.normal(ks[7], (DEPTH, B_CONV, B_CONV_CH), f32) * B_CONV ** -0.5
    conv_b = 0.02 * jax.random.normal(ks[8], (DEPTH, B_CONV_CH), f32)
    dt0 = jnp.exp(jax.random.uniform(ks[9], (DEPTH, B_HEADS), f32,
                                     math.log(1e-3), math.log(1e-1)))
    dt_bias = dt0 + jnp.log(-jnp.expm1(-dt0))
    a_log = jnp.log(jax.random.uniform(ks[10], (DEPTH, B_HEADS), f32, 1.0, 16.0))
    d_skip = 1.0 + 0.02 * jax.random.normal(ks[11], (DEPTH, B_HEADS), f32)
    ssm_norm_w = 1.0 + 0.02 * jax.random.normal(ks[12], (DEPTH, B_WIDTH), f32)
    sinks = 0.5 * jax.random.normal(ks[13], (DEPTH, C_HEADS), f32)
    w_out = jax.random.normal(ks[14], (DEPTH, D_MIX, D_MODEL), f32) * D_MIX ** -0.5
    return {"x": x, "c": c, "ada_w": ada_w, "ada_b": ada_b, "pre_norm_w": pre_norm_w,
            "post_norm_w": post_norm_w, "w_in": w_in, "conv_w": conv_w, "conv_b": conv_b,
            "dt_bias": dt_bias, "a_log": a_log, "d_skip": d_skip, "ssm_norm_w": ssm_norm_w,
            "sinks": sinks, "w_out": w_out}


def reference(x, c, ada_w, ada_b, pre_norm_w, post_norm_w, w_in, conv_w, conv_b,
              dt_bias, a_log, d_skip, ssm_norm_w, sinks, w_out):
    b, s, _ = x.shape
    c_act = jax.nn.silu(c)
    for i in range(DEPTH):
        mod = c_act @ ada_w[i] + ada_b[i]
        shift, scale, gate = jnp.split(mod, 3, axis=-1)
        h = rms_norm(x, pre_norm_w[i]) * (1.0 + scale[:, None]) + shift[:, None]
        proj = h @ w_in[i]
        pa, pb, pc = jnp.split(proj, [A_COLS, A_COLS + B_COLS], axis=-1)

        qa, ka, va, za = jnp.split(pa, 4, axis=-1)
        ya = dilated_attention(qa.reshape(b, s, A_HEADS, HEAD_DIM), ka.reshape(b, s, A_HEADS, HEAD_DIM),
                               va.reshape(b, s, A_HEADS, HEAD_DIM))
        ya = ya.reshape(b, s, A_WIDTH).astype(x.dtype) * jax.nn.silu(za)

        zb, xbc, dtb = jnp.split(pb, [B_WIDTH, B_WIDTH + B_CONV_CH], axis=-1)
        yb = mamba2_mixer(zb, xbc, dtb, conv_w[i], conv_b[i], dt_bias[i], a_log[i],
                          d_skip[i], ssm_norm_w[i])

        qc, zc, kc, vc = jnp.split(pc, [C_WIDTH, 2 * C_WIDTH, 2 * C_WIDTH + C_KV_WIDTH], axis=-1)
        oc, _, _ = banded_window_attention(qc.reshape(b, s, C_KV_HEADS, C_REP, HEAD_DIM),
                                           kc.reshape(b, s, C_KV_HEADS, HEAD_DIM),
                                           vc.reshape(b, s, C_KV_HEADS, HEAD_DIM),
                                           C_WINDOW, sinks[i].reshape(C_KV_HEADS, C_REP))
        yc = oc.reshape(b, s, C_WIDTH).astype(x.dtype) * jax.nn.silu(zc)

        y = jnp.concatenate([ya, yb, yc], axis=-1) @ w_out[i]
        x = x + gate[:, None] * rms_norm(y, post_norm_w[i])
    return x
```

```python
import functools
import math

import numpy as np
import jax
import jax.numpy as jnp
from jax import lax
from jax.experimental import pallas as pl
from jax.experimental.pallas import tpu as pltpu

F32 = jnp.float32
BF16 = jnp.bfloat16

D_MODEL = 1024
DEPTH = 4
HEAD_DIM = 64
LANES = 128
A_WIDTH = 512
A_PATTERNS = ((128, 1), (512, 4), (2048, 16))
B_WIDTH = 1024
B_HEADS = 16
B_GROUPS = 2
B_STATE = 128
B_CONV = 4
B_CONV_CH = B_WIDTH + 2 * B_GROUPS * B_STATE
SSD_CHUNK = 128
C_WIDTH = 512
C_HEADS = 8
C_KV_WIDTH = 128
C_WINDOW = 128
NORM_EPS = 1e-6
RES = 16
BLK = 128
NEG = -1e30
DT_PAD = 128
IN_COLS_PACKED = 4 * A_WIDTH + B_WIDTH + B_CONV_CH + 2 * C_WIDTH + 2 * C_KV_WIDTH + DT_PAD
VMEM_LIMIT = 56 * 1024 * 1024


def _silu(v):
    return v * jax.nn.sigmoid(v)


def _split_dot(x, w_bf16):
    hi = x.astype(BF16)
    lo = (x - hi.astype(F32)).astype(BF16)
    return (jnp.dot(hi, w_bf16, preferred_element_type=F32)
            + jnp.dot(lo, w_bf16, preferred_element_type=F32))


def _mod_kernel(c_ref, w_ref, b_ref, o_ref):
    ca = _silu(c_ref[...])
    o_ref[...] = jnp.dot(ca, w_ref[...], preferred_element_type=F32,
                         precision=lax.Precision.HIGHEST) + b_ref[...]


def _modulation(c, ada_w, ada_b):
    bsz = c.shape[0]
    ada_b4 = ada_b.reshape(DEPTH, 3, 1, D_MODEL)
    return pl.pallas_call(
        _mod_kernel,
        out_shape=jax.ShapeDtypeStruct((DEPTH, bsz, 3 * D_MODEL), F32),
        grid=(DEPTH, 3),
        in_specs=[
            pl.BlockSpec((bsz, D_MODEL), lambda i, k: (0, 0)),
            pl.BlockSpec((None, D_MODEL, D_MODEL), lambda i, k: (i, 0, k)),
            pl.BlockSpec((None, None, 1, D_MODEL), lambda i, k: (i, k, 0, 0)),
        ],
        out_specs=pl.BlockSpec((None, bsz, D_MODEL), lambda i, k: (i, 0, k)),
        compiler_params=pltpu.CompilerParams(dimension_semantics=("arbitrary", "arbitrary")),
        name="ada_mod",
    )(c, ada_w, ada_b4).reshape(DEPTH, bsz, 1, 3 * D_MODEL)


def _inproj_kernel(x_ref, mod_ref, pw_ref, w_ref,
                   qa_ref, ka_ref, va_ref, ga_ref, gb_ref, xbc_ref, dt_ref,
                   qc_ref, gc_ref, kvc_ref):
    x = x_ref[...]
    ms = jnp.mean(x * x, axis=-1, keepdims=True)
    xn = x * lax.rsqrt(ms + NORM_EPS) * pw_ref[...]
    shift, scale_mod = mod_ref[:, 0:D_MODEL], mod_ref[:, D_MODEL:2 * D_MODEL]
    h = (xn * (1.0 + scale_mod) + shift).astype(BF16)

    def proj(c0, width):
        return jnp.dot(h, w_ref[:, c0:c0 + width], preferred_element_type=F32)

    scale = HEAD_DIM ** -0.5
    c0 = 0
    qa_ref[...] = (proj(c0, A_WIDTH) * scale).astype(BF16); c0 += A_WIDTH
    ka_ref[...] = proj(c0, A_WIDTH).astype(BF16); c0 += A_WIDTH
    va_ref[...] = proj(c0, A_WIDTH).astype(BF16); c0 += A_WIDTH
    ga_ref[...] = _silu(proj(c0, A_WIDTH)).astype(BF16); c0 += A_WIDTH
    for half in range(2):
        w = B_WIDTH // 2
        gb_ref[:, half * w:(half + 1) * w] = _silu(proj(c0, w)).astype(BF16); c0 += w
    for part in range(3):
        w = B_CONV_CH // 3
        xbc_ref[:, part * w:(part + 1) * w] = proj(c0, w).astype(BF16); c0 += w
    qc_ref[...] = (proj(c0, C_WIDTH) * scale).astype(BF16); c0 += C_WIDTH
    gc_ref[...] = _silu(proj(c0, C_WIDTH)).astype(BF16); c0 += C_WIDTH
    kvc_ref[...] = proj(c0, 2 * C_KV_WIDTH).astype(BF16); c0 += 2 * C_KV_WIDTH
    dt_ref[...] = proj(c0, DT_PAD)


def _inproj(x4, mod_i, pre_w, w_packed):
    bsz, jn = x4.shape[0], x4.shape[1]

    def strided(width):
        return pl.BlockSpec((None, jn, width), lambda b, r: (b, 0, r))

    def grouped(width):
        return pl.BlockSpec((None, None, jn, width), lambda b, r: (b, r, 0, 0))

    nat = lambda width, dt: jax.ShapeDtypeStruct((bsz, jn, RES * width), dt)
    grp = lambda width, dt: jax.ShapeDtypeStruct((bsz, RES, jn, width), dt)
    return pl.pallas_call(
        _inproj_kernel,
        out_shape=[grp(A_WIDTH, BF16), grp(A_WIDTH, BF16), grp(A_WIDTH, BF16), grp(A_WIDTH, BF16),
                   nat(B_WIDTH, BF16), nat(B_CONV_CH, BF16), nat(DT_PAD, F32),
                   nat(C_WIDTH, BF16), nat(C_WIDTH, BF16), nat(2 * C_KV_WIDTH, BF16)],
        grid=(bsz, RES),
        in_specs=[
            strided(D_MODEL),
            pl.BlockSpec((None, 1, 3 * D_MODEL), lambda b, r: (b, 0, 0)),
            pl.BlockSpec((1, D_MODEL), lambda b, r: (0, 0)),
            pl.BlockSpec((D_MODEL, IN_COLS_PACKED), lambda b, r: (0, 0)),
        ],
        out_specs=[grouped(A_WIDTH), grouped(A_WIDTH), grouped(A_WIDTH), grouped(A_WIDTH),
                   strided(B_WIDTH), strided(B_CONV_CH), strided(DT_PAD),
                   strided(C_WIDTH), strided(C_WIDTH), strided(2 * C_KV_WIDTH)],
        compiler_params=pltpu.CompilerParams(dimension_semantics=("arbitrary", "arbitrary"),
                                             vmem_limit_bytes=VMEM_LIMIT),
        name="in_proj",
    )(x4, mod_i, pre_w, w_packed)


def _head_masks():
    lane = lax.broadcasted_iota(jnp.int32, (BLK, LANES), 1)
    lo = lane < HEAD_DIM
    return lo, lo.astype(F32).astype(BF16), (~lo).astype(F32).astype(BF16)


def _attend_pair(qb, kb, vb, bias2, lo, mlo, mhi):
    qs = jnp.concatenate([qb * mlo, qb * mhi], axis=0)
    s = lax.dot_general(qs, kb, (((1,), (1,)), ((), ())), preferred_element_type=F32) + bias2
    m = jnp.max(s, axis=-1, keepdims=True)
    p = jnp.exp(s - m)
    l = jnp.sum(p, axis=-1, keepdims=True)
    pv = jnp.dot(p.astype(BF16), vb, preferred_element_type=F32)
    acc = jnp.where(lo, pv[:BLK], pv[BLK:])
    mm = jnp.where(lo, m[:BLK], m[BLK:])
    ll = jnp.where(lo, l[:BLK], l[BLK:])
    return mm, ll, acc


def _band_bias(qpos, kpos, window):
    dist = qpos[:, None] - kpos[None, :]
    ok = (dist >= 0) & (dist <= window)
    b = np.where(ok, 0.0, NEG).astype(np.float32)
    return np.concatenate([b, b], axis=0)


def _dilated_biases():
    first, later = [], []
    r, jj = np.meshgrid(np.arange(RES), np.arange(8), indexing="ij")
    qpos = (16 * jj + r).reshape(-1)
    r, jj = np.meshgrid(np.arange(RES), np.arange(16), indexing="ij")
    kpos = (16 * (jj - 8) + r).reshape(-1)
    first.append(_band_bias(qpos, qpos, 128))
    later.append(_band_bias(qpos, kpos, 128))
    rh, jj = np.meshgrid(np.arange(4), np.arange(32), indexing="ij")
    qpos = (4 * jj + rh).reshape(-1)
    rh, jj = np.meshgrid(np.arange(4), np.arange(64), indexing="ij")
    kpos = (4 * (jj - 32) + rh).reshape(-1)
    first.append(_band_bias(qpos, qpos, 128))
    later.append(_band_bias(qpos, kpos, 128))
    qpos = np.arange(BLK)
    kpos = np.arange(2 * BLK) - BLK
    first.append(_band_bias(qpos, qpos, 128))
    later.append(_band_bias(qpos, kpos, 128))
    return jnp.asarray(np.stack(first)), jnp.asarray(np.stack(later))


def _dilated_kernel(q_ref, k_ref, v_ref, g_ref, b1_ref, b2_ref, o_ref,
                    qf, kf, vf, m1, l1, a1, m2, l2, a2, m3, l3, a3):
    jn = q_ref.shape[1]
    lo, mlo, mhi = _head_masks()
    attend = functools.partial(_attend_pair, lo=lo, mlo=mlo, mhi=mhi)

    def p3_body(r, carry):
        mm, ll, acc = attend(q_ref[r, 0:BLK, :], k_ref[r, 0:BLK, :], v_ref[r, 0:BLK, :], b1_ref[2])
        m3[r, 0:BLK, :] = mm; l3[r, 0:BLK, :] = ll; a3[r, 0:BLK, :] = acc
        for jb in range(1, jn // BLK):
            s0 = (jb - 1) * BLK
            mm, ll, acc = attend(q_ref[r, s0 + BLK:s0 + 2 * BLK, :], k_ref[r, s0:s0 + 2 * BLK, :],
                                 v_ref[r, s0:s0 + 2 * BLK, :], b2_ref[2])
            m3[r, s0 + BLK:s0 + 2 * BLK, :] = mm
            l3[r, s0 + BLK:s0 + 2 * BLK, :] = ll
            a3[r, s0 + BLK:s0 + 2 * BLK, :] = acc
        return carry
    lax.fori_loop(0, RES, p3_body, 0)

    def p2_rows(ref, r4, start, size):
        return jnp.concatenate([ref[4 * rh + r4, pl.ds(start, size), :] for rh in range(4)], axis=0)

    def p2_store(r4, start, mm, ll, acc):
        for rh in range(4):
            sl = slice(32 * rh, 32 * rh + 32)
            m2[4 * rh + r4, pl.ds(start, 32), :] = mm[sl]
            l2[4 * rh + r4, pl.ds(start, 32), :] = ll[sl]
            a2[4 * rh + r4, pl.ds(start, 32), :] = acc[sl]

    for r4 in range(4):
        mm, ll, acc = attend(p2_rows(q_ref, r4, 0, 32), p2_rows(k_ref, r4, 0, 32),
                             p2_rows(v_ref, r4, 0, 32), b1_ref[1])
        p2_store(r4, 0, mm, ll, acc)

        def p2_body(b4, carry, r4=r4):
            q0 = pl.multiple_of(b4 * 32, 32)
            k0 = pl.multiple_of(b4 * 32 - 32, 32)
            mm, ll, acc = attend(p2_rows(q_ref, r4, q0, 32), p2_rows(k_ref, r4, k0, 64),
                                 p2_rows(v_ref, r4, k0, 64), b2_ref[1])
            p2_store(r4, q0, mm, ll, acc)
            return carry
        lax.fori_loop(1, jn // 32, p2_body, 0)

    for r in range(RES):
        qf[r] = q_ref[r].astype(F32)
        kf[r] = k_ref[r].astype(F32)
        vf[r] = v_ref[r].astype(F32)

    def p1_rows(ref, start, size):
        return jnp.concatenate([ref[r, pl.ds(start, size), :] for r in range(RES)], axis=0).astype(BF16)

    def p1_store(start, mm, ll, acc):
        for r in range(RES):
            sl = slice(8 * r, 8 * r + 8)
            m1[r, pl.ds(start, 8), :] = mm[sl]
            l1[r, pl.ds(start, 8), :] = ll[sl]
            a1[r, pl.ds(start, 8), :] = acc[sl]

    mm, ll, acc = attend(p1_rows(qf, 0, 8), p1_rows(kf, 0, 8), p1_rows(vf, 0, 8), b1_ref[0])
    p1_store(0, mm, ll, acc)

    def p1_body(a, carry):
        q0 = pl.multiple_of(a * 8, 8)
        k0 = pl.multiple_of(a * 8 - 8, 8)
        mm, ll, acc = attend(p1_rows(qf, q0, 8), p1_rows(kf, k0, 16), p1_rows(vf, k0, 16), b2_ref[0])
        p1_store(q0, mm, ll, acc)
        return carry
    lax.fori_loop(1, jn // 8, p1_body, 0)

    def merge_body(r, carry):
        ma, mb, mc = m1[r], m2[r], m3[r]
        mx = jnp.maximum(jnp.maximum(ma, mb), mc)
        ea, eb, ec = jnp.exp(ma - mx), jnp.exp(mb - mx), jnp.exp(mc - mx)
        num = ea * a1[r] + eb * a2[r] + ec * a3[r]
        den = ea * l1[r] + eb * l2[r] + ec * l3[r]
        o_ref[r] = (num / den * g_ref[r].astype(F32)).astype(BF16)
        return carry
    lax.fori_loop(0, RES, merge_body, 0)


def _dilated_attention(qa, ka, va, ga, bias_first, bias_later):
    bsz, _, jn, _ = qa.shape
    spec = pl.BlockSpec((None, RES, jn, LANES), lambda b, hp: (b, 0, 0, hp))
    acc = pltpu.VMEM((RES, jn, LANES), F32)
    return pl.pallas_call(
        _dilated_kernel,
        out_shape=jax.ShapeDtypeStruct(qa.shape, BF16),
        grid=(bsz, A_WIDTH // LANES),
        in_specs=[spec, spec, spec, spec,
                  pl.BlockSpec(bias_first.shape, lambda b, hp: (0, 0, 0)),
                  pl.BlockSpec(bias_later.shape, lambda b, hp: (0, 0, 0))],
        out_specs=spec,
        scratch_shapes=[acc] * 12,
        compiler_params=pltpu.CompilerParams(dimension_semantics=("arbitrary", "arbitrary"),
                                             vmem_limit_bytes=VMEM_LIMIT),
        name="dilated_attn",
    )(qa, ka, va, ga, bias_first, bias_later)


C_TILE = 512


def _sink_kernel(q_ref, g_ref, kv_ref, kvp_ref, sink_ref, bias_ref, o_ref):
    first_tile = pl.program_id(1) == 0
    lane = lax.broadcasted_iota(jnp.int32, (BLK, LANES), 1)
    lo = lane < HEAD_DIM
    mlo = lo.astype(F32).astype(BF16)
    mhi = (~lo).astype(F32).astype(BF16)
    ngrp = C_WIDTH // LANES
    sink = sink_ref[...]
    for u in range(C_TILE // BLK):
        r0 = u * BLK
        cur = kv_ref[r0:r0 + BLK, :]
        if u == 0:
            prev = kvp_ref[...]
            bias = bias_ref[jnp.where(first_tile, 0, 1)]
        else:
            prev = kv_ref[r0 - BLK:r0, :]
            bias = bias_ref[1]
        kb = jnp.concatenate([prev[:, :LANES], cur[:, :LANES]], axis=0)
        vb = jnp.concatenate([prev[:, LANES:], cur[:, LANES:]], axis=0)
        qg = [q_ref[r0:r0 + BLK, g * LANES:(g + 1) * LANES] for g in range(ngrp)]
        qs = jnp.concatenate([q * mlo for q in qg] + [q * mhi for q in qg], axis=0)
        s = lax.dot_general(qs, kb, (((1,), (1,)), ((), ())), preferred_element_type=F32)
        s = (s.reshape(2 * ngrp, BLK, 2 * BLK) + bias[None]).reshape(2 * ngrp * BLK, 2 * BLK)
        m = jnp.maximum(jnp.max(s, axis=-1, keepdims=True), sink)
        p = jnp.exp(s - m)
        l = jnp.sum(p, axis=-1, keepdims=True) + jnp.exp(sink - m)
        pv = jnp.dot(p.astype(BF16), vb, preferred_element_type=F32) / l
        for g in range(ngrp):
            og = jnp.where(lo, pv[g * BLK:(g + 1) * BLK], pv[(ngrp + g) * BLK:(ngrp + g + 1) * BLK])
            gate = g_ref[r0:r0 + BLK, g * LANES:(g + 1) * LANES].astype(F32)
            o_ref[r0:r0 + BLK, g * LANES:(g + 1) * LANES] = (og * gate).astype(BF16)


def _sink_attention(qc, gc, kvc, sink_col, bias):
    bsz, seq, _ = qc.shape
    per = C_TILE // BLK
    return pl.pallas_call(
        _sink_kernel,
        out_shape=jax.ShapeDtypeStruct(qc.shape, BF16),
        grid=(bsz, seq // C_TILE),
        in_specs=[
            pl.BlockSpec((None, C_TILE, C_WIDTH), lambda b, i: (b, i, 0)),
            pl.BlockSpec((None, C_TILE, C_WIDTH), lambda b, i: (b, i, 0)),
            pl.BlockSpec((None, C_TILE, 2 * C_KV_WIDTH), lambda b, i: (b, i, 0)),
            pl.BlockSpec((None, BLK, 2 * C_KV_WIDTH), lambda b, i: (b, jnp.maximum(i * per - 1, 0), 0)),
            pl.BlockSpec(sink_col.shape, lambda b, i: (0, 0)),
            pl.BlockSpec(bias.shape, lambda b, i: (0, 0, 0)),
        ],
        out_specs=pl.BlockSpec((None, C_TILE, C_WIDTH), lambda b, i: (b, i, 0)),
        compiler_params=pltpu.CompilerParams(dimension_semantics=("arbitrary", "arbitrary"),
                                             vmem_limit_bytes=VMEM_LIMIT),
        name="sink_attn",
    )(qc, gc, kvc, kvc, sink_col, bias)


SSD_TILE = 512
CONV_PAD = 8


def _ssd_kernel(xbc_ref, gz_ref, dt_ref, cw_ref, cb_ref, dtb_ref, alog_ref, dexp_ref, nw_ref,
                expand_ref, tri_ref, o_ref, xpad, state):
    q = SSD_CHUNK
    hg = B_HEADS // B_GROUPS
    gw = B_WIDTH // B_GROUPS

    @pl.when(pl.program_id(1) == 0)
    def _():
        xpad[0:CONV_PAD, :] = jnp.zeros((CONV_PAD, B_CONV_CH), F32)
        state[...] = jnp.zeros_like(state)

    xpad[CONV_PAD:CONV_PAD + SSD_TILE, :] = xbc_ref[...].astype(F32)
    tri = tri_ref[...]
    causal = tri > 0
    expand = expand_ref[...]

    for c in range(SSD_TILE // q):
        t0 = c * q
        conv = cb_ref[...] + sum(
            cw_ref[k:k + 1, :] * xpad[CONV_PAD + t0 - (B_CONV - 1) + k:CONV_PAD + t0 - (B_CONV - 1) + k + q, :]
            for k in range(B_CONV))
        xc = _silu(conv)
        xs = xc[:, :B_WIDTH]
        dt = jax.nn.softplus(dt_ref[t0:t0 + q, :] + dtb_ref[...])
        da = dt * -jnp.exp(alog_ref[...])
        hi = da.astype(BF16)
        lo_part = (da - hi.astype(F32)).astype(BF16)
        a_cum = (jnp.dot(tri, hi, preferred_element_type=F32)
                 + jnp.dot(tri, lo_part, preferred_element_type=F32))
        a_cum_t = a_cum.T
        a_last = a_cum[q - 1:q, :]
        ea = jnp.exp(a_cum)
        dte = jnp.exp(a_last - a_cum)
        dt_x = _split_dot(dt, expand)
        ea_x = _split_dot(ea, expand)
        dte_x = _split_dot(dte, expand)
        xdt = xs * dt_x
        xdt_b = xdt.astype(BF16)
        xdte_b = (xdt * dte_x).astype(BF16)
        decay_x = ea_x[q - 1:q, :]

        y_parts = []
        for g in range(B_GROUPS):
            bm = xc[:, B_WIDTH + g * B_STATE:B_WIDTH + (g + 1) * B_STATE]
            cm = xc[:, B_WIDTH + (B_GROUPS + g) * B_STATE:B_WIDTH + (B_GROUPS + g + 1) * B_STATE]
            bm_b, cm_b = bm.astype(BF16), cm.astype(BF16)
            cbm = lax.dot_general(cm_b, bm_b, (((1,), (1,)), ((), ())), preferred_element_type=F32)
            st = state[:, g * gw:(g + 1) * gw]
            y_off = jnp.dot(cm_b, st.astype(BF16), preferred_element_type=F32) * ea_x[:, g * gw:(g + 1) * gw]
            new = jnp.dot(bm.T.astype(BF16), xdte_b[:, g * gw:(g + 1) * gw], preferred_element_type=F32)
            state[:, g * gw:(g + 1) * gw] = st * decay_x[:, g * gw:(g + 1) * gw] + new
            lane = lax.broadcasted_iota(jnp.int32, (q, LANES), 1)
            lo = lane < HEAD_DIM
            y_diag = []
            for pr in range(hg // 2):
                h0 = g * hg + 2 * pr
                mats = []
                for h in (h0, h0 + 1):
                    seg = a_cum[:, h:h + 1] - a_cum_t[h:h + 1, :]
                    mats.append((cbm * jnp.exp(jnp.where(causal, seg, NEG))).astype(BF16))
                lhs = jnp.concatenate(mats, axis=0)
                c0 = (h0 // 2) * LANES
                yy = jnp.dot(lhs, xdt_b[:, c0:c0 + LANES], preferred_element_type=F32)
                y_diag.append(jnp.where(lo, yy[:q], yy[q:]))
            y_parts.append(jnp.concatenate(y_diag, axis=1) + y_off)
        y = jnp.concatenate(y_parts, axis=1) + dexp_ref[...] * xs
        y = y * gz_ref[t0:t0 + q, :].astype(F32)
        outs = []
        for g in range(B_GROUPS):
            yg = y[:, g * gw:(g + 1) * gw]
            outs.append(yg * lax.rsqrt(jnp.mean(yg * yg, axis=-1, keepdims=True) + NORM_EPS))
        o_ref[t0:t0 + q, :] = (jnp.concatenate(outs, axis=1) * nw_ref[...]).astype(BF16)

    xpad[0:CONV_PAD, :] = xpad[SSD_TILE:SSD_TILE + CONV_PAD, :]


def _ssd(xbc, gzb, dt_raw, conv_w, conv_b, dt_bias_row, a_row, d_exp, norm_w, expand, tri):
    bsz, seq, _ = xbc.shape
    const = lambda a: pl.BlockSpec(a.shape, lambda b, i: (0,) * a.ndim)
    return pl.pallas_call(
        _ssd_kernel,
        out_shape=jax.ShapeDtypeStruct((bsz, seq, B_WIDTH), BF16),
        grid=(bsz, seq // SSD_TILE),
        in_specs=[
            pl.BlockSpec((None, SSD_TILE, B_CONV_CH), lambda b, i: (b, i, 0)),
            pl.BlockSpec((None, SSD_TILE, B_WIDTH), lambda b, i: (b, i, 0)),
            pl.BlockSpec((None, SSD_TILE, DT_PAD), lambda b, i: (b, i, 0)),
            const(conv_w), const(conv_b), const(dt_bias_row), const(a_row), const(d_exp), const(norm_w),
            const(expand), const(tri),
        ],
        out_specs=pl.BlockSpec((None, SSD_TILE, B_WIDTH), lambda b, i: (b, i, 0)),
        scratch_shapes=[pltpu.VMEM((CONV_PAD + SSD_TILE, B_CONV_CH), F32),
                        pltpu.VMEM((B_STATE, B_WIDTH), F32)],
        compiler_params=pltpu.CompilerParams(dimension_semantics=("arbitrary", "arbitrary"),
                                             vmem_limit_bytes=VMEM_LIMIT),
        name="ssd",
    )(xbc, gzb, dt_raw, conv_w, conv_b, dt_bias_row, a_row, d_exp, norm_w, expand, tri)


def _outproj_kernel(ya_ref, yb_ref, yc_ref, x_ref, mod_ref, pw_ref, w_ref, o_ref):
    y = jnp.dot(ya_ref[...], w_ref[0:A_WIDTH, :], preferred_element_type=F32)
    y += jnp.dot(yb_ref[...], w_ref[A_WIDTH:A_WIDTH + B_WIDTH, :], preferred_element_type=F32)
    y += jnp.dot(yc_ref[...], w_ref[A_WIDTH + B_WIDTH:, :], preferred_element_type=F32)
    ms = jnp.mean(y * y, axis=-1, keepdims=True)
    yn = y * lax.rsqrt(ms + NORM_EPS) * pw_ref[...]
    o_ref[...] = x_ref[...] + mod_ref[:, 2 * D_MODEL:] * yn


def _outproj(ya, yb4, yc4, x4, mod_i, post_w, w_out_packed):
    bsz, jn = x4.shape[0], x4.shape[1]
    strided = lambda width: pl.BlockSpec((None, jn, width), lambda b, r: (b, 0, r))
    return pl.pallas_call(
        _outproj_kernel,
        out_shape=jax.ShapeDtypeStruct(x4.shape, F32),
        grid=(bsz, RES),
        in_specs=[
            pl.BlockSpec((None, None, jn, A_WIDTH), lambda b, r: (b, r, 0, 0)),
            strided(B_WIDTH), strided(C_WIDTH), strided(D_MODEL),
            pl.BlockSpec((None, 1, 3 * D_MODEL), lambda b, r: (b, 0, 0)),
            pl.BlockSpec((1, D_MODEL), lambda b, r: (0, 0)),
            pl.BlockSpec(w_out_packed.shape, lambda b, r: (0, 0)),
        ],
        out_specs=strided(D_MODEL),
        compiler_params=pltpu.CompilerParams(dimension_semantics=("arbitrary", "arbitrary"),
                                             vmem_limit_bytes=VMEM_LIMIT),
        name="out_proj",
    )(ya, yb4, yc4, x4, mod_i, post_w, w_out_packed)


def _c_head_perm():
    ngrp = C_WIDTH // LANES
    cols = []
    for g in range(ngrp):
        for half in range(2):
            h = g + ngrp * half
            cols.extend(range(h * HEAD_DIM, (h + 1) * HEAD_DIM))
    return np.asarray(cols)


def _pack_w_in(w):
    a_cols = 4 * A_WIDTH
    zb0 = a_cols
    xbc0 = zb0 + B_WIDTH
    dt0 = xbc0 + B_CONV_CH
    c0 = dt0 + B_HEADS
    perm = _c_head_perm()
    qc = w[:, c0:c0 + C_WIDTH][:, perm]
    zc = w[:, c0 + C_WIDTH:c0 + 2 * C_WIDTH][:, perm]
    kv = w[:, c0 + 2 * C_WIDTH:c0 + 2 * C_WIDTH + 2 * C_KV_WIDTH]
    dt = jnp.pad(w[:, dt0:dt0 + B_HEADS], ((0, 0), (0, DT_PAD - B_HEADS)))
    return jnp.concatenate([w[:, :dt0], qc, zc, kv, dt], axis=1).astype(BF16)


def _pack_w_out(w):
    perm = _c_head_perm()
    c0 = A_WIDTH + B_WIDTH
    return jnp.concatenate([w[:c0], w[c0:][perm]], axis=0).astype(BF16)


def kernel(x, c, ada_w, ada_b, pre_norm_w, post_norm_w, w_in, conv_w, conv_b, dt_bias, a_log, d_skip,
           ssm_norm_w, sinks, w_out):
    bsz, seq, _ = x.shape
    jn = seq // RES
    assert seq % (RES * BLK) == 0 and seq % SSD_TILE == 0 and seq % C_TILE == 0

    mod = _modulation(c, ada_w, ada_b)
    bias_first, bias_later = _dilated_biases()
    kpos = np.arange(2 * BLK) - BLK
    swa = _band_bias(np.arange(BLK), kpos, C_WINDOW)[:BLK]
    swa_first = np.where(kpos[None, :] >= 0, swa, NEG).astype(np.float32)
    bias_c = jnp.asarray(np.stack([swa_first, swa]))
    expand_np = np.zeros((LANES, B_WIDTH), np.float32)
    for h in range(B_HEADS):
        expand_np[h, h * HEAD_DIM:(h + 1) * HEAD_DIM] = 1.0
    expand = jnp.asarray(expand_np).astype(BF16)
    tri = jnp.asarray(np.tril(np.ones((SSD_CHUNK, SSD_CHUNK), np.float32))).astype(BF16)

    x4 = x.reshape(bsz, jn, RES * D_MODEL)
    for i in range(DEPTH):
        w_packed = _pack_w_in(w_in[i])
        qa, ka, va, ga, gzb, xbc, dt_raw, qc, gc, kvc = _inproj(
            x4, mod[i], pre_norm_w[i].reshape(1, D_MODEL), w_packed)

        ya = _dilated_attention(qa, ka, va, ga, bias_first, bias_later)

        pad16 = lambda v: jnp.pad(v.astype(F32), (0, DT_PAD - B_HEADS)).reshape(1, DT_PAD)
        yb = _ssd(xbc.reshape(bsz, seq, B_CONV_CH), gzb.reshape(bsz, seq, B_WIDTH),
                  dt_raw.reshape(bsz, seq, DT_PAD), conv_w[i], conv_b[i].reshape(1, B_CONV_CH),
                  pad16(dt_bias[i]), pad16(a_log[i]),
                  jnp.repeat(d_skip[i].astype(F32), HEAD_DIM).reshape(1, B_WIDTH),
                  ssm_norm_w[i].reshape(1, B_WIDTH), expand, tri)

        sink_col = jnp.repeat(sinks[i].astype(F32), BLK).reshape(C_HEADS * BLK, 1)
        yc = _sink_attention(qc.reshape(bsz, seq, C_WIDTH), gc.reshape(bsz, seq, C_WIDTH),
                             kvc.reshape(bsz, seq, 2 * C_KV_WIDTH), sink_col, bias_c)

        x4 = _outproj(ya, yb.reshape(bsz, jn, RES * B_WIDTH), yc.reshape(bsz, jn, RES * C_WIDTH),
                      x4, mod[i], post_norm_w[i].reshape(1, D_MODEL), _pack_w_out(w_out[i]))
    return x4.reshape(bsz, seq, D_MODEL)
```

```python
import functools
import math

import numpy as np
import jax
import jax.numpy as jnp
from jax import lax
from jax.experimental import pallas as pl
from jax.experimental.pallas import tpu as pltpu

F32 = jnp.float32
BF16 = jnp.bfloat16

D_MODEL = 1024
DEPTH = 4
HEAD_DIM = 64
LANES = 128
BF16_ROWS = 16
A_WIDTH = 512
B_WIDTH = 1024
B_HEADS = 16
B_GROUPS = 2
B_STATE = 128
B_CONV = 4
B_CONV_CH = B_WIDTH + 2 * B_GROUPS * B_STATE
SSD_CHUNK = 128
C_WIDTH = 512
C_HEADS = 8
C_KV_WIDTH = 128
C_WINDOW = 128
A_WINDOW = 128
NORM_EPS = 1e-6
RES = 16
BLK = 128
NEG = -1e30
LOG2E = math.log2(math.e)
DT_PAD = 128
IN_COLS_PACKED = 4 * A_WIDTH + B_WIDTH + B_CONV_CH + 2 * C_WIDTH + 2 * C_KV_WIDTH + DT_PAD
ROW_TILE = RES * BF16_ROWS
VMEM_LIMIT = 56 * 1024 * 1024
UNROLL = 4


def _silu(v):
    return v * jax.nn.sigmoid(v)


def _split_dot(x, w_bf16):
    hi = x.astype(BF16)
    lo = (x - hi.astype(F32)).astype(BF16)
    return (jnp.dot(hi, w_bf16, preferred_element_type=F32)
            + jnp.dot(lo, w_bf16, preferred_element_type=F32))


def _residue_perm():
    p = np.zeros((ROW_TILE, ROW_TILE), np.float32)
    for r in range(RES):
        for jj in range(BF16_ROWS):
            p[r * BF16_ROWS + jj, RES * jj + r] = 1.0
    return p


def _mod_kernel(c_ref, w_ref, b_ref, o_ref):
    ca = _silu(c_ref[...])
    o_ref[...] = jnp.dot(ca, w_ref[...], preferred_element_type=F32,
                         precision=lax.Precision.HIGHEST) + b_ref[...]


def _modulation(c, ada_w, ada_b):
    bsz = c.shape[0]
    ada_b4 = ada_b.reshape(DEPTH, 3, 1, D_MODEL)
    return pl.pallas_call(
        _mod_kernel,
        out_shape=jax.ShapeDtypeStruct((DEPTH, bsz, 3 * D_MODEL), F32),
        grid=(DEPTH, 3),
        in_specs=[
            pl.BlockSpec((bsz, D_MODEL), lambda i, k: (0, 0)),
            pl.BlockSpec((None, D_MODEL, D_MODEL), lambda i, k: (i, 0, k)),
            pl.BlockSpec((None, None, 1, D_MODEL), lambda i, k: (i, k, 0, 0)),
        ],
        out_specs=pl.BlockSpec((None, bsz, D_MODEL), lambda i, k: (i, 0, k)),
        compiler_params=pltpu.CompilerParams(dimension_semantics=("arbitrary", "arbitrary")),
        name="ada_mod",
    )(c, ada_w, ada_b4).reshape(DEPTH, bsz, 1, 3 * D_MODEL)


def _inproj_kernel(x_ref, mod_ref, pw_ref, w_ref, perm_ref,
                   qa_ref, ka_ref, va_ref, ga_ref, gb_ref, xbc_ref, dt_ref,
                   qc_ref, gc_ref, kvc_ref):
    x = x_ref[...]
    ms = jnp.mean(x * x, axis=-1, keepdims=True)
    xn = x * lax.rsqrt(ms + NORM_EPS) * pw_ref[...]
    shift, scale_mod = mod_ref[:, 0:D_MODEL], mod_ref[:, D_MODEL:2 * D_MODEL]
    h = (xn * (1.0 + scale_mod) + shift).astype(BF16)
    h_res = jnp.dot(perm_ref[...], h, preferred_element_type=F32).astype(BF16)

    def proj(lhs, c0, width):
        return jnp.dot(lhs, w_ref[:, c0:c0 + width], preferred_element_type=F32)

    def store_grouped(ref, val):
        for r in range(RES):
            ref[r] = val[r * BF16_ROWS:(r + 1) * BF16_ROWS]

    qscale = HEAD_DIM ** -0.5 * LOG2E
    c0 = 0
    store_grouped(qa_ref, (proj(h_res, c0, A_WIDTH) * qscale).astype(BF16)); c0 += A_WIDTH
    store_grouped(ka_ref, proj(h_res, c0, A_WIDTH).astype(BF16)); c0 += A_WIDTH
    store_grouped(va_ref, proj(h_res, c0, A_WIDTH).astype(BF16)); c0 += A_WIDTH
    store_grouped(ga_ref, _silu(proj(h_res, c0, A_WIDTH)).astype(BF16)); c0 += A_WIDTH
    for half in range(2):
        w = B_WIDTH // 2
        gb_ref[:, half * w:(half + 1) * w] = _silu(proj(h, c0, w)).astype(BF16); c0 += w
    for part in range(3):
        w = B_CONV_CH // 3
        xbc_ref[:, part * w:(part + 1) * w] = proj(h, c0, w).astype(BF16); c0 += w
    qc_ref[...] = (proj(h, c0, C_WIDTH) * qscale).astype(BF16); c0 += C_WIDTH
    gc_ref[...] = _silu(proj(h, c0, C_WIDTH)).astype(BF16); c0 += C_WIDTH
    kvc_ref[...] = proj(h, c0, 2 * C_KV_WIDTH).astype(BF16); c0 += 2 * C_KV_WIDTH
    dt_ref[...] = proj(h, c0, DT_PAD)


def _inproj(x, mod_i, pre_w, w_packed, perm):
    bsz, seq, _ = x.shape
    jn = seq // RES
    nat_spec = lambda width: pl.BlockSpec((None, ROW_TILE, width), lambda b, i: (b, i, 0))
    grp_spec = pl.BlockSpec((None, RES, BF16_ROWS, A_WIDTH), lambda b, i: (b, 0, i, 0))
    nat = lambda width, dt: jax.ShapeDtypeStruct((bsz, seq, width), dt)
    grp = jax.ShapeDtypeStruct((bsz, RES, jn, A_WIDTH), BF16)
    return pl.pallas_call(
        _inproj_kernel,
        out_shape=[grp, grp, grp, grp,
                   nat(B_WIDTH, BF16), nat(B_CONV_CH, BF16), nat(DT_PAD, F32),
                   nat(C_WIDTH, BF16), nat(C_WIDTH, BF16), nat(2 * C_KV_WIDTH, BF16)],
        grid=(bsz, seq // ROW_TILE),
        in_specs=[
            nat_spec(D_MODEL),
            pl.BlockSpec((None, 1, 3 * D_MODEL), lambda b, i: (b, 0, 0)),
            pl.BlockSpec((1, D_MODEL), lambda b, i: (0, 0)),
            pl.BlockSpec((D_MODEL, IN_COLS_PACKED), lambda b, i: (0, 0)),
            pl.BlockSpec((ROW_TILE, ROW_TILE), lambda b, i: (0, 0)),
        ],
        out_specs=[grp_spec, grp_spec, grp_spec, grp_spec,
                   nat_spec(B_WIDTH), nat_spec(B_CONV_CH), nat_spec(DT_PAD),
                   nat_spec(C_WIDTH), nat_spec(C_WIDTH), nat_spec(2 * C_KV_WIDTH)],
        compiler_params=pltpu.CompilerParams(dimension_semantics=("arbitrary", "arbitrary"),
                                             vmem_limit_bytes=VMEM_LIMIT),
        name="in_proj",
    )(x, mod_i, pre_w, w_packed, perm)


def _head_masks():
    lane = lax.broadcasted_iota(jnp.int32, (BLK, LANES), 1)
    lo = lane < HEAD_DIM
    return lo, lo.astype(F32).astype(BF16), (~lo).astype(F32).astype(BF16)


def _attend_pairs(blocks, lo, mlo, mhi):
    scores = []
    for qb, kb, _, bias2 in blocks:
        qs = jnp.concatenate([qb * mlo, qb * mhi], axis=0)
        scores.append(lax.dot_general(qs, kb, (((1,), (1,)), ((), ())),
                                      preferred_element_type=F32) + bias2)
    outs = []
    for (_, _, vb, _), s in zip(blocks, scores):
        m = jnp.max(s, axis=-1, keepdims=True)
        p = jnp.exp2(s - m)
        l = jnp.sum(p, axis=-1, keepdims=True)
        pv = jnp.dot(p.astype(BF16), vb, preferred_element_type=F32)
        outs.append((jnp.where(lo, m[:BLK], m[BLK:]), jnp.where(lo, l[:BLK], l[BLK:]),
                     jnp.where(lo, pv[:BLK], pv[BLK:])))
    return outs


def _band_bias(qpos, kpos, window):
    dist = qpos[:, None] - kpos[None, :]
    ok = (dist >= 0) & (dist <= window)
    b = np.where(ok, 0.0, NEG).astype(np.float32)
    return np.concatenate([b, b], axis=0)


def _dilated_biases():
    first, later = [], []
    r, jj = np.meshgrid(np.arange(RES), np.arange(8), indexing="ij")
    qpos = (16 * jj + r).reshape(-1)
    r, jj = np.meshgrid(np.arange(RES), np.arange(16), indexing="ij")
    kpos = (16 * (jj - 8) + r).reshape(-1)
    first.append(_band_bias(qpos, qpos, A_WINDOW))
    later.append(_band_bias(qpos, kpos, A_WINDOW))
    rh, jj = np.meshgrid(np.arange(4), np.arange(32), indexing="ij")
    qpos = (4 * jj + rh).reshape(-1)
    rh, jj = np.meshgrid(np.arange(4), np.arange(64), indexing="ij")
    kpos = (4 * (jj - 32) + rh).reshape(-1)
    first.append(_band_bias(qpos, qpos, A_WINDOW))
    later.append(_band_bias(qpos, kpos, A_WINDOW))
    qpos = np.arange(BLK)
    kpos = np.arange(2 * BLK) - BLK
    first.append(_band_bias(qpos, qpos, A_WINDOW))
    later.append(_band_bias(qpos, kpos, A_WINDOW))
    return jnp.asarray(np.stack(first)), jnp.asarray(np.stack(later))


def _dilated_kernel(q_ref, k_ref, v_ref, g_ref, b1_ref, b2_ref, o_ref,
                    qf, kf, vf, m1, l1, a1, m2, l2, a2, m3, l3, a3):
    jn = q_ref.shape[1]
    lo, mlo, mhi = _head_masks()
    attend = functools.partial(_attend_pairs, lo=lo, mlo=mlo, mhi=mhi)

    def p3_blocks(r):
        blocks = [(q_ref[r, 0:BLK, :], k_ref[r, 0:BLK, :], v_ref[r, 0:BLK, :], b1_ref[2])]
        for jb in range(1, jn // BLK):
            s0 = (jb - 1) * BLK
            blocks.append((q_ref[r, s0 + BLK:s0 + 2 * BLK, :], k_ref[r, s0:s0 + 2 * BLK, :],
                           v_ref[r, s0:s0 + 2 * BLK, :], b2_ref[2]))
        return blocks

    def p3_store(r, outs):
        for jb, (mm, ll, acc) in enumerate(outs):
            m3[r, jb * BLK:(jb + 1) * BLK, :] = mm
            l3[r, jb * BLK:(jb + 1) * BLK, :] = ll
            a3[r, jb * BLK:(jb + 1) * BLK, :] = acc

    def p3_body(rr, carry):
        per = jn // BLK
        outs = attend(p3_blocks(2 * rr) + p3_blocks(2 * rr + 1))
        p3_store(2 * rr, outs[:per])
        p3_store(2 * rr + 1, outs[per:])
        return carry
    lax.fori_loop(0, RES // 2, p3_body, 0)

    def p2_rows(ref, r4, start, size):
        return jnp.concatenate([ref[4 * rh + r4, pl.ds(start, size), :] for rh in range(4)], axis=0)

    def p2_store(r4, start, out):
        mm, ll, acc = out
        for rh in range(4):
            sl = slice(32 * rh, 32 * rh + 32)
            m2[4 * rh + r4, pl.ds(start, 32), :] = mm[sl]
            l2[4 * rh + r4, pl.ds(start, 32), :] = ll[sl]
            a2[4 * rh + r4, pl.ds(start, 32), :] = acc[sl]

    outs = attend([(p2_rows(q_ref, r4, 0, 32), p2_rows(k_ref, r4, 0, 32), p2_rows(v_ref, r4, 0, 32),
                    b1_ref[1]) for r4 in range(4)])
    for r4 in range(4):
        p2_store(r4, 0, outs[r4])

    def p2_body(b4, carry):
        q0 = pl.multiple_of(b4 * 32, 32)
        k0 = pl.multiple_of(b4 * 32 - 32, 32)
        outs = attend([(p2_rows(q_ref, r4, q0, 32), p2_rows(k_ref, r4, k0, 64), p2_rows(v_ref, r4, k0, 64),
                        b2_ref[1]) for r4 in range(4)])
        for r4 in range(4):
            p2_store(r4, q0, outs[r4])
        return carry
    lax.fori_loop(1, jn // 32, p2_body, 0)

    for r in range(RES):
        qf[r] = q_ref[r].astype(F32)
        kf[r] = k_ref[r].astype(F32)
        vf[r] = v_ref[r].astype(F32)

    def p1_rows(ref, start, size):
        return jnp.concatenate([ref[r, pl.ds(start, size), :] for r in range(RES)], axis=0).astype(BF16)

    def p1_store(start, out):
        mm, ll, acc = out
        for r in range(RES):
            sl = slice(8 * r, 8 * r + 8)
            m1[r, pl.ds(start, 8), :] = mm[sl]
            l1[r, pl.ds(start, 8), :] = ll[sl]
            a1[r, pl.ds(start, 8), :] = acc[sl]

    def p1_later(q0):
        k0 = q0 - 8 if isinstance(q0, int) else pl.multiple_of(q0 - 8, 8)
        return (p1_rows(qf, q0, 8), p1_rows(kf, k0, 16), p1_rows(vf, k0, 16), b2_ref[0])

    outs = attend([(p1_rows(qf, 0, 8), p1_rows(kf, 0, 8), p1_rows(vf, 0, 8), b1_ref[0])]
                  + [p1_later(8 * a) for a in range(1, UNROLL)])
    for a in range(UNROLL):
        p1_store(8 * a, outs[a])

    def p1_body(i, carry):
        starts = [pl.multiple_of((i * UNROLL + d) * 8, 8) for d in range(UNROLL)]
        outs = attend([p1_later(q0) for q0 in starts])
        for q0, out in zip(starts, outs):
            p1_store(q0, out)
        return carry
    lax.fori_loop(1, jn // (8 * UNROLL), p1_body, 0)

    def merge_body(r, carry):
        ma, mb, mc = m1[r], m2[r], m3[r]
        mx = jnp.maximum(jnp.maximum(ma, mb), mc)
        ea, eb, ec = jnp.exp2(ma - mx), jnp.exp2(mb - mx), jnp.exp2(mc - mx)
        num = ea * a1[r] + eb * a2[r] + ec * a3[r]
        den = ea * l1[r] + eb * l2[r] + ec * l3[r]
        o_ref[r] = (num / den * g_ref[r].astype(F32)).astype(BF16)
        return carry
    lax.fori_loop(0, RES, merge_body, 0)


def _dilated_attention(qa, ka, va, ga, bias_first, bias_later):
    bsz, _, jn, _ = qa.shape
    assert jn % (8 * UNROLL) == 0 and jn % BLK == 0
    spec = pl.BlockSpec((None, RES, jn, LANES), lambda b, hp: (b, 0, 0, hp))
    acc = pltpu.VMEM((RES, jn, LANES), F32)
    return pl.pallas_call(
        _dilated_kernel,
        out_shape=jax.ShapeDtypeStruct(qa.shape, BF16),
        grid=(bsz, A_WIDTH // LANES),
        in_specs=[spec, spec, spec, spec,
                  pl.BlockSpec(bias_first.shape, lambda b, hp: (0, 0, 0)),
                  pl.BlockSpec(bias_later.shape, lambda b, hp: (0, 0, 0))],
        out_specs=spec,
        scratch_shapes=[acc] * 12,
        compiler_params=pltpu.CompilerParams(dimension_semantics=("arbitrary", "arbitrary"),
                                             vmem_limit_bytes=VMEM_LIMIT),
        name="dilated_attn",
    )(qa, ka, va, ga, bias_first, bias_later)


C_TILE = 512


def _sink_kernel(q_ref, g_ref, kv_ref, kvp_ref, sink_ref, bias_ref, o_ref):
    first_tile = pl.program_id(1) == 0
    lo, mlo, mhi = _head_masks()
    ngrp = C_WIDTH // LANES
    sink = sink_ref[...]
    for u in range(C_TILE // BLK):
        r0 = u * BLK
        cur = kv_ref[r0:r0 + BLK, :]
        if u == 0:
            prev = kvp_ref[...]
            bias = bias_ref[jnp.where(first_tile, 0, 1)]
        else:
            prev = kv_ref[r0 - BLK:r0, :]
            bias = bias_ref[1]
        kb = jnp.concatenate([prev[:, :LANES], cur[:, :LANES]], axis=0)
        vb = jnp.concatenate([prev[:, LANES:], cur[:, LANES:]], axis=0)
        qg = [q_ref[r0:r0 + BLK, g * LANES:(g + 1) * LANES] for g in range(ngrp)]
        qs = jnp.concatenate([q * mlo for q in qg] + [q * mhi for q in qg], axis=0)
        s = lax.dot_general(qs, kb, (((1,), (1,)), ((), ())), preferred_element_type=F32)
        s = (s.reshape(2 * ngrp, BLK, 2 * BLK) + bias[None]).reshape(2 * ngrp * BLK, 2 * BLK)
        m = jnp.maximum(jnp.max(s, axis=-1, keepdims=True), sink)
        p = jnp.exp2(s - m)
        l = jnp.sum(p, axis=-1, keepdims=True) + jnp.exp2(sink - m)
        pv = jnp.dot(p.astype(BF16), vb, preferred_element_type=F32) / l
        for g in range(ngrp):
            og = jnp.where(lo, pv[g * BLK:(g + 1) * BLK], pv[(ngrp + g) * BLK:(ngrp + g + 1) * BLK])
            gate = g_ref[r0:r0 + BLK, g * LANES:(g + 1) * LANES].astype(F32)
            o_ref[r0:r0 + BLK, g * LANES:(g + 1) * LANES] = (og * gate).astype(BF16)


def _sink_attention(qc, gc, kvc, sink_col, bias):
    bsz, seq, _ = qc.shape
    per = C_TILE // BLK
    return pl.pallas_call(
        _sink_kernel,
        out_shape=jax.ShapeDtypeStruct(qc.shape, BF16),
        grid=(bsz, seq // C_TILE),
        in_specs=[
            pl.BlockSpec((None, C_TILE, C_WIDTH), lambda b, i: (b, i, 0)),
            pl.BlockSpec((None, C_TILE, C_WIDTH), lambda b, i: (b, i, 0)),
            pl.BlockSpec((None, C_TILE, 2 * C_KV_WIDTH), lambda b, i: (b, i, 0)),
            pl.BlockSpec((None, BLK, 2 * C_KV_WIDTH), lambda b, i: (b, jnp.maximum(i * per - 1, 0), 0)),
            pl.BlockSpec(sink_col.shape, lambda b, i: (0, 0)),
            pl.BlockSpec(bias.shape, lambda b, i: (0, 0, 0)),
        ],
        out_specs=pl.BlockSpec((None, C_TILE, C_WIDTH), lambda b, i: (b, i, 0)),
        compiler_params=pltpu.CompilerParams(dimension_semantics=("arbitrary", "arbitrary"),
                                             vmem_limit_bytes=VMEM_LIMIT),
        name="sink_attn",
    )(qc, gc, kvc, kvc, sink_col, bias)


SSD_TILE = 512
CONV_PAD = 8


def _ssd_kernel(xbc_ref, gz_ref, dt_ref, cw_ref, cb_ref, dtb_ref, alog_ref, dexp_ref, nw_ref,
                expand_ref, tri_ref, o_ref, xpad, state):
    q = SSD_CHUNK
    hg = B_HEADS // B_GROUPS
    gw = B_WIDTH // B_GROUPS

    @pl.when(pl.program_id(1) == 0)
    def _():
        xpad[0:CONV_PAD, :] = jnp.zeros((CONV_PAD, B_CONV_CH), F32)
        state[...] = jnp.zeros_like(state)

    xpad[CONV_PAD:CONV_PAD + SSD_TILE, :] = xbc_ref[...].astype(F32)
    tri = tri_ref[...]
    causal = tri > 0
    expand = expand_ref[...]
    lo = lax.broadcasted_iota(jnp.int32, (q, LANES), 1) < HEAD_DIM
    a2 = -jnp.exp(alog_ref[...]) * LOG2E

    for c in range(SSD_TILE // q):
        t0 = c * q
        conv = cb_ref[...] + sum(
            cw_ref[k:k + 1, :] * xpad[CONV_PAD + t0 - (B_CONV - 1) + k:CONV_PAD + t0 - (B_CONV - 1) + k + q, :]
            for k in range(B_CONV))
        xc = _silu(conv)
        xs = xc[:, :B_WIDTH]
        dt = jax.nn.softplus(dt_ref[t0:t0 + q, :] + dtb_ref[...])
        da = dt * a2
        a_cum = _split_dot_left(tri, da)
        a_cum_t = a_cum.T
        dt_t = dt.T
        ea = jnp.exp2(a_cum)
        dte = jnp.exp2(a_cum[q - 1:q, :] - a_cum)
        ea_x = _split_dot(ea, expand)
        w_x = jnp.dot((dt * dte).astype(BF16), expand, preferred_element_type=F32)
        xs_b = xs.astype(BF16)
        xw_b = (xs * w_x).astype(BF16)
        decay_x = ea_x[q - 1:q, :]

        y_parts = []
        for g in range(B_GROUPS):
            bm = xc[:, B_WIDTH + g * B_STATE:B_WIDTH + (g + 1) * B_STATE]
            cm = xc[:, B_WIDTH + (B_GROUPS + g) * B_STATE:B_WIDTH + (B_GROUPS + g + 1) * B_STATE]
            bm_b, cm_b = bm.astype(BF16), cm.astype(BF16)
            cbm = lax.dot_general(cm_b, bm_b, (((1,), (1,)), ((), ())), preferred_element_type=F32)
            st = state[:, g * gw:(g + 1) * gw]
            y_off = jnp.dot(cm_b, st.astype(BF16), preferred_element_type=F32) * ea_x[:, g * gw:(g + 1) * gw]
            new = jnp.dot(bm.T.astype(BF16), xw_b[:, g * gw:(g + 1) * gw], preferred_element_type=F32)
            state[:, g * gw:(g + 1) * gw] = st * decay_x[:, g * gw:(g + 1) * gw] + new
            y_diag = []
            for pr in range(hg // 2):
                h0 = g * hg + 2 * pr
                mats = []
                for h in (h0, h0 + 1):
                    seg = a_cum[:, h:h + 1] - a_cum_t[h:h + 1, :]
                    decay = jnp.exp2(jnp.where(causal, seg, NEG))
                    mats.append((cbm * decay * dt_t[h:h + 1, :]).astype(BF16))
                lhs = jnp.concatenate(mats, axis=0)
                c0 = (h0 // 2) * LANES
                yy = jnp.dot(lhs, xs_b[:, c0:c0 + LANES], preferred_element_type=F32)
                y_diag.append(jnp.where(lo, yy[:q], yy[q:]))
            y_parts.append(jnp.concatenate(y_diag, axis=1) + y_off)
        y = jnp.concatenate(y_parts, axis=1) + dexp_ref[...] * xs
        y = y * gz_ref[t0:t0 + q, :].astype(F32)
        outs = []
        for g in range(B_GROUPS):
            yg = y[:, g * gw:(g + 1) * gw]
            outs.append(yg * lax.rsqrt(jnp.mean(yg * yg, axis=-1, keepdims=True) + NORM_EPS))
        o_ref[t0:t0 + q, :] = (jnp.concatenate(outs, axis=1) * nw_ref[...]).astype(BF16)

    xpad[0:CONV_PAD, :] = xpad[SSD_TILE:SSD_TILE + CONV_PAD, :]


def _split_dot_left(w_bf16, x):
    hi = x.astype(BF16)
    lo = (x - hi.astype(F32)).astype(BF16)
    return (jnp.dot(w_bf16, hi, preferred_element_type=F32)
            + jnp.dot(w_bf16, lo, preferred_element_type=F32))


def _ssd(xbc, gzb, dt_raw, conv_w, conv_b, dt_bias_row, a_log_row, d_exp, norm_w, expand, tri):
    bsz, seq, _ = xbc.shape
    const = lambda a: pl.BlockSpec(a.shape, lambda b, i: (0,) * a.ndim)
    return pl.pallas_call(
        _ssd_kernel,
        out_shape=jax.ShapeDtypeStruct((bsz, seq, B_WIDTH), BF16),
        grid=(bsz, seq // SSD_TILE),
        in_specs=[
            pl.BlockSpec((None, SSD_TILE, B_CONV_CH), lambda b, i: (b, i, 0)),
            pl.BlockSpec((None, SSD_TILE, B_WIDTH), lambda b, i: (b, i, 0)),
            pl.BlockSpec((None, SSD_TILE, DT_PAD), lambda b, i: (b, i, 0)),
            const(conv_w), const(conv_b), const(dt_bias_row), const(a_log_row), const(d_exp), const(norm_w),
            const(expand), const(tri),
        ],
        out_specs=pl.BlockSpec((None, SSD_TILE, B_WIDTH), lambda b, i: (b, i, 0)),
        scratch_shapes=[pltpu.VMEM((CONV_PAD + SSD_TILE, B_CONV_CH), F32),
                        pltpu.VMEM((B_STATE, B_WIDTH), F32)],
        compiler_params=pltpu.CompilerParams(dimension_semantics=("arbitrary", "arbitrary"),
                                             vmem_limit_bytes=VMEM_LIMIT),
        name="ssd",
    )(xbc, gzb, dt_raw, conv_w, conv_b, dt_bias_row, a_log_row, d_exp, norm_w, expand, tri)


def _outproj_kernel(ya_ref, yb_ref, yc_ref, x_ref, mod_ref, pw_ref, w_ref, permt_ref, o_ref):
    ya_res = jnp.concatenate([ya_ref[r] for r in range(RES)], axis=0)
    ya = jnp.dot(permt_ref[...], ya_res, preferred_element_type=F32).astype(BF16)
    y = jnp.dot(ya, w_ref[0:A_WIDTH, :], preferred_element_type=F32)
    y += jnp.dot(yb_ref[...], w_ref[A_WIDTH:A_WIDTH + B_WIDTH, :], preferred_element_type=F32)
    y += jnp.dot(yc_ref[...], w_ref[A_WIDTH + B_WIDTH:, :], preferred_element_type=F32)
    ms = jnp.mean(y * y, axis=-1, keepdims=True)
    yn = y * lax.rsqrt(ms + NORM_EPS) * pw_ref[...]
    o_ref[...] = x_ref[...] + mod_ref[:, 2 * D_MODEL:] * yn


def _outproj(ya, yb, yc, x, mod_i, post_w, w_out_packed, perm_t):
    bsz, seq, _ = x.shape
    nat_spec = lambda width: pl.BlockSpec((None, ROW_TILE, width), lambda b, i: (b, i, 0))
    return pl.pallas_call(
        _outproj_kernel,
        out_shape=jax.ShapeDtypeStruct(x.shape, F32),
        grid=(bsz, seq // ROW_TILE),
        in_specs=[
            pl.BlockSpec((None, RES, BF16_ROWS, A_WIDTH), lambda b, i: (b, 0, i, 0)),
            nat_spec(B_WIDTH), nat_spec(C_WIDTH), nat_spec(D_MODEL),
            pl.BlockSpec((None, 1, 3 * D_MODEL), lambda b, i: (b, 0, 0)),
            pl.BlockSpec((1, D_MODEL), lambda b, i: (0, 0)),
            pl.BlockSpec(w_out_packed.shape, lambda b, i: (0, 0)),
            pl.BlockSpec((ROW_TILE, ROW_TILE), lambda b, i: (0, 0)),
        ],
        out_specs=nat_spec(D_MODEL),
        compiler_params=pltpu.CompilerParams(dimension_semantics=("arbitrary", "arbitrary"),
                                             vmem_limit_bytes=VMEM_LIMIT),
        name="out_proj",
    )(ya, yb, yc, x, mod_i, post_w, w_out_packed, perm_t)


def _c_head_perm():
    ngrp = C_WIDTH // LANES
    cols = []
    for g in range(ngrp):
        for half in range(2):
            h = g + ngrp * half
            cols.extend(range(h * HEAD_DIM, (h + 1) * HEAD_DIM))
    return np.asarray(cols)


def _pack_w_in(w):
    a_cols = 4 * A_WIDTH
    zb0 = a_cols
    xbc0 = zb0 + B_WIDTH
    dt0 = xbc0 + B_CONV_CH
    c0 = dt0 + B_HEADS
    perm = _c_head_perm()
    qc = w[:, c0:c0 + C_WIDTH][:, perm]
    zc = w[:, c0 + C_WIDTH:c0 + 2 * C_WIDTH][:, perm]
    kv = w[:, c0 + 2 * C_WIDTH:c0 + 2 * C_WIDTH + 2 * C_KV_WIDTH]
    dt = jnp.pad(w[:, dt0:dt0 + B_HEADS], ((0, 0), (0, DT_PAD - B_HEADS)))
    return jnp.concatenate([w[:, :dt0], qc, zc, kv, dt], axis=1).astype(BF16)


def _pack_w_out(w):
    perm = _c_head_perm()
    c0 = A_WIDTH + B_WIDTH
    return jnp.concatenate([w[:c0], w[c0:][perm]], axis=0).astype(BF16)


def kernel(x, c, ada_w, ada_b, pre_norm_w, post_norm_w, w_in, conv_w, conv_b, dt_bias, a_log, d_skip,
           ssm_norm_w, sinks, w_out):
    bsz, seq, _ = x.shape
    assert seq % (RES * BLK) == 0 and seq % SSD_TILE == 0 and seq % C_TILE == 0

    mod = _modulation(c, ada_w, ada_b)
    bias_first, bias_later = _dilated_biases()
    kpos = np.arange(2 * BLK) - BLK
    swa = _band_bias(np.arange(BLK), kpos, C_WINDOW)[:BLK]
    swa_first = np.where(kpos[None, :] >= 0, swa, NEG).astype(np.float32)
    bias_c = jnp.asarray(np.stack([swa_first, swa]))
    expand_np = np.zeros((LANES, B_WIDTH), np.float32)
    for h in range(B_HEADS):
        expand_np[h, h * HEAD_DIM:(h + 1) * HEAD_DIM] = 1.0
    expand = jnp.asarray(expand_np).astype(BF16)
    tri = jnp.asarray(np.tril(np.ones((SSD_CHUNK, SSD_CHUNK), np.float32))).astype(BF16)
    perm_np = _residue_perm()
    perm = jnp.asarray(perm_np).astype(BF16)
    perm_t = jnp.asarray(perm_np.T).astype(BF16)
    pad16 = lambda v: jnp.pad(v.astype(F32), (0, DT_PAD - B_HEADS)).reshape(1, DT_PAD)

    for i in range(DEPTH):
        qa, ka, va, ga, gzb, xbc, dt_raw, qc, gc, kvc = _inproj(
            x, mod[i], pre_norm_w[i].reshape(1, D_MODEL), _pack_w_in(w_in[i]), perm)

        ya = _dilated_attention(qa, ka, va, ga, bias_first, bias_later)

        yb = _ssd(xbc, gzb, dt_raw, conv_w[i], conv_b[i].reshape(1, B_CONV_CH),
                  pad16(dt_bias[i]), pad16(a_log[i]),
                  jnp.repeat(d_skip[i].astype(F32), HEAD_DIM).reshape(1, B_WIDTH),
                  ssm_norm_w[i].reshape(1, B_WIDTH), expand, tri)

        sink_col = jnp.repeat(sinks[i].astype(F32) * LOG2E, BLK).reshape(C_HEADS * BLK, 1)
        yc = _sink_attention(qc, gc, kvc, sink_col, bias_c)

        x = _outproj(ya, yb, yc, x, mod[i], post_norm_w[i].reshape(1, D_MODEL),
                     _pack_w_out(w_out[i]), perm_t)
    return x
```

```python
import functools
import math

import numpy as np
import jax
import jax.numpy as jnp
from jax import lax
from jax.experimental import pallas as pl
from jax.experimental.pallas import tpu as pltpu

F32 = jnp.float32
BF16 = jnp.bfloat16

D_MODEL = 1024
DEPTH = 4
HEAD_DIM = 64
LANES = 128
BF16_ROWS = 16
A_WIDTH = 512
B_WIDTH = 1024
B_HEADS = 16
B_GROUPS = 2
B_STATE = 128
B_CONV = 4
B_CONV_CH = B_WIDTH + 2 * B_GROUPS * B_STATE
SSD_CHUNK = 128
C_WIDTH = 512
C_HEADS = 8
C_KV_WIDTH = 128
C_WINDOW = 128
A_WINDOW = 128
NORM_EPS = 1e-6
RES = 16
BLK = 128
NEG = -1e30
LOG2E = math.log2(math.e)
DT_PAD = 128
IN_COLS_PACKED = 4 * A_WIDTH + B_WIDTH + B_CONV_CH + 2 * C_WIDTH + 2 * C_KV_WIDTH + DT_PAD
PERM_TILE = RES * BF16_ROWS
PERM_PER_TILE = 2
ROW_TILE = PERM_TILE * PERM_PER_TILE
VMEM_LIMIT = 56 * 1024 * 1024
UNROLL = 8


def _silu(v):
    return v * jax.nn.sigmoid(v)


def _split_dot(x, w_bf16):
    hi = x.astype(BF16)
    lo = (x - hi.astype(F32)).astype(BF16)
    return (jnp.dot(hi, w_bf16, preferred_element_type=F32)
            + jnp.dot(lo, w_bf16, preferred_element_type=F32))


def _residue_perm():
    p = np.zeros((PERM_TILE, PERM_TILE), np.float32)
    for r in range(RES):
        for jj in range(BF16_ROWS):
            p[r * BF16_ROWS + jj, RES * jj + r] = 1.0
    return p


def _mod_kernel(c_ref, w_ref, b_ref, o_ref):
    ca = _silu(c_ref[...])
    o_ref[...] = jnp.dot(ca, w_ref[...], preferred_element_type=F32,
                         precision=lax.Precision.HIGHEST) + b_ref[...]


def _modulation(c, ada_w, ada_b):
    bsz = c.shape[0]
    ada_b4 = ada_b.reshape(DEPTH, 3, 1, D_MODEL)
    return pl.pallas_call(
        _mod_kernel,
        out_shape=jax.ShapeDtypeStruct((DEPTH, bsz, 3 * D_MODEL), F32),
        grid=(DEPTH, 3),
        in_specs=[
            pl.BlockSpec((bsz, D_MODEL), lambda i, k: (0, 0)),
            pl.BlockSpec((None, D_MODEL, D_MODEL), lambda i, k: (i, 0, k)),
            pl.BlockSpec((None, None, 1, D_MODEL), lambda i, k: (i, k, 0, 0)),
        ],
        out_specs=pl.BlockSpec((None, bsz, D_MODEL), lambda i, k: (i, 0, k)),
        compiler_params=pltpu.CompilerParams(dimension_semantics=("arbitrary", "arbitrary")),
        name="ada_mod",
    )(c, ada_w, ada_b4).reshape(DEPTH, bsz, 1, 3 * D_MODEL)


def _inproj_kernel(x_ref, mod_ref, pw_ref, w_ref, perm_ref, cw_ref, cb_ref,
                   qa_ref, ka_ref, va_ref, ga_ref, gb_ref, xc_ref, dt_ref,
                   qc_ref, gc_ref, kvc_ref, xpad):
    x = x_ref[...]
    ms = jnp.mean(x * x, axis=-1, keepdims=True)
    xn = x * lax.rsqrt(ms + NORM_EPS) * pw_ref[...]
    shift, scale_mod = mod_ref[:, 0:D_MODEL], mod_ref[:, D_MODEL:2 * D_MODEL]
    h = (xn * (1.0 + scale_mod) + shift).astype(BF16)
    h_res = jnp.concatenate(
        [jnp.dot(perm_ref[...], h[u * PERM_TILE:(u + 1) * PERM_TILE], preferred_element_type=F32).astype(BF16)
         for u in range(PERM_PER_TILE)], axis=0)

    def proj(lhs, c0, width):
        return jnp.dot(lhs, w_ref[:, c0:c0 + width], preferred_element_type=F32)

    def store_grouped(ref, val):
        for u in range(PERM_PER_TILE):
            for r in range(RES):
                row = u * PERM_TILE + r * BF16_ROWS
                ref[r, u * BF16_ROWS:(u + 1) * BF16_ROWS, :] = val[row:row + BF16_ROWS]

    qscale = HEAD_DIM ** -0.5 * LOG2E
    off_zb = 4 * A_WIDTH
    off_xbc = off_zb + B_WIDTH
    off_qc = off_xbc + B_CONV_CH
    off_kv = off_qc + 2 * C_WIDTH
    off_dt = off_kv + 2 * C_KV_WIDTH
    pw = B_CONV_CH // 3

    @pl.when(pl.program_id(1) == 0)
    def _():
        xpad[0:CONV_PAD, :] = jnp.zeros((CONV_PAD, B_CONV_CH), F32)
    for part in range(3):
        xpad[CONV_PAD:CONV_PAD + ROW_TILE, part * pw:(part + 1) * pw] = proj(h, off_xbc + part * pw, pw)

    for part in range(3):
        cols = slice(part * pw, (part + 1) * pw)
        conv = cb_ref[:, cols] + sum(
            cw_ref[k:k + 1, cols] * xpad[CONV_PAD - (B_CONV - 1) + k:CONV_PAD - (B_CONV - 1) + k + ROW_TILE, cols]
            for k in range(B_CONV))
        xc_ref[:, cols] = _silu(conv).astype(BF16)
    xpad[0:CONV_PAD, :] = xpad[ROW_TILE:ROW_TILE + CONV_PAD, :]

    store_grouped(ga_ref, _silu(proj(h_res, 3 * A_WIDTH, A_WIDTH)).astype(BF16))
    for half in range(2):
        w = B_WIDTH // 2
        gb_ref[:, half * w:(half + 1) * w] = _silu(proj(h, off_zb + half * w, w)).astype(BF16)
    gc_ref[...] = _silu(proj(h, off_qc + C_WIDTH, C_WIDTH)).astype(BF16)
    store_grouped(qa_ref, (proj(h_res, 0, A_WIDTH) * qscale).astype(BF16))
    qc_ref[...] = (proj(h, off_qc, C_WIDTH) * qscale).astype(BF16)
    store_grouped(ka_ref, proj(h_res, A_WIDTH, A_WIDTH).astype(BF16))
    store_grouped(va_ref, proj(h_res, 2 * A_WIDTH, A_WIDTH).astype(BF16))
    kvc_ref[...] = proj(h, off_kv, 2 * C_KV_WIDTH).astype(BF16)
    dt_ref[...] = proj(h, off_dt, DT_PAD)


CONV_PAD = 8


def _inproj(x, mod_i, pre_w, w_packed, perm, conv_w, conv_b):
    bsz, seq, _ = x.shape
    jn = seq // RES
    nat_spec = lambda width: pl.BlockSpec((None, ROW_TILE, width), lambda b, i: (b, i, 0))
    grp_spec = pl.BlockSpec((None, RES, BF16_ROWS * PERM_PER_TILE, A_WIDTH), lambda b, i: (b, 0, i, 0))
    nat = lambda width, dt: jax.ShapeDtypeStruct((bsz, seq, width), dt)
    grp = jax.ShapeDtypeStruct((bsz, RES, jn, A_WIDTH), BF16)
    return pl.pallas_call(
        _inproj_kernel,
        out_shape=[grp, grp, grp, grp,
                   nat(B_WIDTH, BF16), nat(B_CONV_CH, BF16), nat(DT_PAD, F32),
                   nat(C_WIDTH, BF16), nat(C_WIDTH, BF16), nat(2 * C_KV_WIDTH, BF16)],
        grid=(bsz, seq // ROW_TILE),
        in_specs=[
            nat_spec(D_MODEL),
            pl.BlockSpec((None, 1, 3 * D_MODEL), lambda b, i: (b, 0, 0)),
            pl.BlockSpec((1, D_MODEL), lambda b, i: (0, 0)),
            pl.BlockSpec((D_MODEL, IN_COLS_PACKED), lambda b, i: (0, 0), pipeline_mode=pl.Buffered(1)),
            pl.BlockSpec((PERM_TILE, PERM_TILE), lambda b, i: (0, 0)),
            pl.BlockSpec(conv_w.shape, lambda b, i: (0, 0)),
            pl.BlockSpec(conv_b.shape, lambda b, i: (0, 0)),
        ],
        out_specs=[grp_spec, grp_spec, grp_spec, grp_spec,
                   nat_spec(B_WIDTH), nat_spec(B_CONV_CH), nat_spec(DT_PAD),
                   nat_spec(C_WIDTH), nat_spec(C_WIDTH), nat_spec(2 * C_KV_WIDTH)],
        scratch_shapes=[pltpu.VMEM((CONV_PAD + ROW_TILE, B_CONV_CH), F32)],
        compiler_params=pltpu.CompilerParams(dimension_semantics=("arbitrary", "arbitrary"),
                                             vmem_limit_bytes=VMEM_LIMIT),
        name="in_proj",
    )(x, mod_i, pre_w, w_packed, perm, conv_w, conv_b)


def _head_masks():
    lane = lax.broadcasted_iota(jnp.int32, (BLK, LANES), 1)
    lo = lane < HEAD_DIM
    return lo, lo.astype(F32).astype(BF16), (~lo).astype(F32).astype(BF16)


def _attend_pairs(blocks, lo, mlo, mhi):
    scores = []
    for qb, kb, _, bias2 in blocks:
        qs = jnp.concatenate([qb * mlo, qb * mhi], axis=0)
        scores.append(lax.dot_general(qs, kb, (((1,), (1,)), ((), ())),
                                      preferred_element_type=F32) + bias2)
    outs = []
    for (_, _, vb, _), s in zip(blocks, scores):
        m = jnp.max(s, axis=-1, keepdims=True)
        p = jnp.exp2(s - m).astype(BF16)
        v_ext = jnp.concatenate([vb, jnp.ones_like(vb)], axis=1)
        pv = jnp.dot(p, v_ext, preferred_element_type=F32)
        outs.append((jnp.where(lo, m[:BLK], m[BLK:]),
                     jnp.where(lo, pv[:BLK, LANES:], pv[BLK:, LANES:]),
                     jnp.where(lo, pv[:BLK, :LANES], pv[BLK:, :LANES])))
    return outs


def _band_bias(qpos, kpos, window):
    dist = qpos[:, None] - kpos[None, :]
    ok = (dist >= 0) & (dist <= window)
    b = np.where(ok, 0.0, NEG).astype(np.float32)
    return np.concatenate([b, b], axis=0)


def _dilated_biases():
    first, later = [], []
    r, jj = np.meshgrid(np.arange(RES), np.arange(8), indexing="ij")
    qpos = (16 * jj + r).reshape(-1)
    r, jj = np.meshgrid(np.arange(RES), np.arange(16), indexing="ij")
    kpos = (16 * (jj - 8) + r).reshape(-1)
    first.append(_band_bias(qpos, qpos, A_WINDOW))
    later.append(_band_bias(qpos, kpos, A_WINDOW))
    rh, jj = np.meshgrid(np.arange(4), np.arange(32), indexing="ij")
    qpos = (4 * jj + rh).reshape(-1)
    rh, jj = np.meshgrid(np.arange(4), np.arange(64), indexing="ij")
    kpos = (4 * (jj - 32) + rh).reshape(-1)
    first.append(_band_bias(qpos, qpos, A_WINDOW))
    later.append(_band_bias(qpos, kpos, A_WINDOW))
    qpos = np.arange(BLK)
    kpos = np.arange(2 * BLK) - BLK
    first.append(_band_bias(qpos, qpos, A_WINDOW))
    later.append(_band_bias(qpos, kpos, A_WINDOW))
    return jnp.asarray(np.stack(first)), jnp.asarray(np.stack(later))


def _dilated_kernel(q_ref, k_ref, v_ref, g_ref, b1_ref, b2_ref, o_ref,
                    qf, kf, vf, m1, l1, a1, m2, l2, a2, m3, l3, a3):
    jn = q_ref.shape[1]
    lo, mlo, mhi = _head_masks()
    attend = functools.partial(_attend_pairs, lo=lo, mlo=mlo, mhi=mhi)

    def p3_blocks(r):
        blocks = [(q_ref[r, 0:BLK, :], k_ref[r, 0:BLK, :], v_ref[r, 0:BLK, :], b1_ref[2])]
        for jb in range(1, jn // BLK):
            s0 = (jb - 1) * BLK
            blocks.append((q_ref[r, s0 + BLK:s0 + 2 * BLK, :], k_ref[r, s0:s0 + 2 * BLK, :],
                           v_ref[r, s0:s0 + 2 * BLK, :], b2_ref[2]))
        return blocks

    def p3_store(r, outs):
        for jb, (mm, ll, acc) in enumerate(outs):
            m3[r, jb * BLK:(jb + 1) * BLK, :] = mm
            l3[r, jb * BLK:(jb + 1) * BLK, :] = ll
            a3[r, jb * BLK:(jb + 1) * BLK, :] = acc

    per3 = jn // BLK
    res3 = UNROLL // per3

    def p3_body(rr, carry):
        rs = [rr * res3 + d for d in range(res3)]
        outs = attend([blk for r in rs for blk in p3_blocks(r)])
        for d, r in enumerate(rs):
            p3_store(r, outs[d * per3:(d + 1) * per3])
        return carry
    lax.fori_loop(0, RES // res3, p3_body, 0)

    def p2_rows(ref, r4, start, size):
        return jnp.concatenate([ref[4 * rh + r4, pl.ds(start, size), :] for rh in range(4)], axis=0)

    def p2_store(r4, start, out):
        mm, ll, acc = out
        for rh in range(4):
            sl = slice(32 * rh, 32 * rh + 32)
            m2[4 * rh + r4, pl.ds(start, 32), :] = mm[sl]
            l2[4 * rh + r4, pl.ds(start, 32), :] = ll[sl]
            a2[4 * rh + r4, pl.ds(start, 32), :] = acc[sl]

    def p2_first(r4):
        return (p2_rows(q_ref, r4, 0, 32), p2_rows(k_ref, r4, 0, 32), p2_rows(v_ref, r4, 0, 32), b1_ref[1])

    def p2_later(r4, q0):
        k0 = q0 - 32 if isinstance(q0, int) else pl.multiple_of(q0 - 32, 32)
        return (p2_rows(q_ref, r4, q0, 32), p2_rows(k_ref, r4, k0, 64), p2_rows(v_ref, r4, k0, 64), b2_ref[1])

    steps2 = UNROLL // 4
    outs = attend([p2_first(r4) for r4 in range(4)]
                  + [p2_later(r4, 32 * b4) for b4 in range(1, steps2) for r4 in range(4)])
    for b4 in range(steps2):
        for r4 in range(4):
            p2_store(r4, 32 * b4, outs[4 * b4 + r4])

    def p2_body(i, carry):
        starts = [pl.multiple_of((i * steps2 + d) * 32, 32) for d in range(steps2)]
        outs = attend([p2_later(r4, q0) for q0 in starts for r4 in range(4)])
        for d, q0 in enumerate(starts):
            for r4 in range(4):
                p2_store(r4, q0, outs[4 * d + r4])
        return carry
    lax.fori_loop(1, jn // (32 * steps2), p2_body, 0)

    for r in range(RES):
        qf[r] = q_ref[r].astype(F32)
        kf[r] = k_ref[r].astype(F32)
        vf[r] = v_ref[r].astype(F32)

    def p1_rows(ref, start, size):
        return jnp.concatenate([ref[r, pl.ds(start, size), :] for r in range(RES)], axis=0).astype(BF16)

    def p1_store(start, out):
        mm, ll, acc = out
        for r in range(RES):
            sl = slice(8 * r, 8 * r + 8)
            m1[r, pl.ds(start, 8), :] = mm[sl]
            l1[r, pl.ds(start, 8), :] = ll[sl]
            a1[r, pl.ds(start, 8), :] = acc[sl]

    def p1_later(q0):
        k0 = q0 - 8 if isinstance(q0, int) else pl.multiple_of(q0 - 8, 8)
        return (p1_rows(qf, q0, 8), p1_rows(kf, k0, 16), p1_rows(vf, k0, 16), b2_ref[0])

    outs = attend([(p1_rows(qf, 0, 8), p1_rows(kf, 0, 8), p1_rows(vf, 0, 8), b1_ref[0])]
                  + [p1_later(8 * a) for a in range(1, UNROLL)])
    for a in range(UNROLL):
        p1_store(8 * a, outs[a])

    def p1_body(i, carry):
        starts = [pl.multiple_of((i * UNROLL + d) * 8, 8) for d in range(UNROLL)]
        outs = attend([p1_later(q0) for q0 in starts])
        for q0, out in zip(starts, outs):
            p1_store(q0, out)
        return carry
    lax.fori_loop(1, jn // (8 * UNROLL), p1_body, 0)

    def merge_body(r, carry):
        ma, mb, mc = m1[r], m2[r], m3[r]
        mx = jnp.maximum(jnp.maximum(ma, mb), mc)
        ea, eb, ec = jnp.exp2(ma - mx), jnp.exp2(mb - mx), jnp.exp2(mc - mx)
        num = ea * a1[r] + eb * a2[r] + ec * a3[r]
        den = ea * l1[r] + eb * l2[r] + ec * l3[r]
        o_ref[r] = (num / den * g_ref[r].astype(F32)).astype(BF16)
        return carry
    lax.fori_loop(0, RES, merge_body, 0)


def _dilated_attention(qa, ka, va, ga, bias_first, bias_later):
    bsz, _, jn, _ = qa.shape
    assert jn % (8 * UNROLL) == 0 and jn % BLK == 0
    spec = pl.BlockSpec((None, RES, jn, LANES), lambda b, hp: (b, 0, 0, hp))
    acc = pltpu.VMEM((RES, jn, LANES), F32)
    return pl.pallas_call(
        _dilated_kernel,
        out_shape=jax.ShapeDtypeStruct(qa.shape, BF16),
        grid=(bsz, A_WIDTH // LANES),
        in_specs=[spec, spec, spec, spec,
                  pl.BlockSpec(bias_first.shape, lambda b, hp: (0, 0, 0)),
                  pl.BlockSpec(bias_later.shape, lambda b, hp: (0, 0, 0))],
        out_specs=spec,
        scratch_shapes=[acc] * 12,
        compiler_params=pltpu.CompilerParams(dimension_semantics=("arbitrary", "arbitrary"),
                                             vmem_limit_bytes=VMEM_LIMIT),
        name="dilated_attn",
    )(qa, ka, va, ga, bias_first, bias_later)


C_TILE = 512


def _sink_kernel(q_ref, g_ref, kv_ref, kvp_ref, sink_ref, bias_ref, o_ref):
    first_tile = pl.program_id(1) == 0
    lo, mlo, mhi = _head_masks()
    ngrp = C_WIDTH // LANES
    per_call = UNROLL // ngrp
    for u0 in range(0, C_TILE // BLK, per_call):
        blocks = []
        for u in range(u0, u0 + per_call):
            r0 = u * BLK
            cur = kv_ref[r0:r0 + BLK, :]
            if u == 0:
                prev = kvp_ref[...]
                bias = bias_ref[jnp.where(first_tile, 0, 1)]
            else:
                prev = kv_ref[r0 - BLK:r0, :]
                bias = bias_ref[1]
            kb = jnp.concatenate([prev[:, :LANES], cur[:, :LANES]], axis=0)
            vb = jnp.concatenate([prev[:, LANES:], cur[:, LANES:]], axis=0)
            for g in range(ngrp):
                blocks.append((q_ref[r0:r0 + BLK, g * LANES:(g + 1) * LANES], kb, vb, bias))
        outs = _attend_pairs(blocks, lo, mlo, mhi)
        for idx, (mm, ll, acc) in enumerate(outs):
            r0 = (u0 + idx // ngrp) * BLK
            cols = slice((idx % ngrp) * LANES, (idx % ngrp + 1) * LANES)
            sk = sink_ref[idx % ngrp]
            m2 = jnp.maximum(mm, sk)
            e = jnp.exp2(mm - m2)
            den = ll * e + jnp.exp2(sk - m2)
            gate = g_ref[r0:r0 + BLK, cols].astype(F32)
            o_ref[r0:r0 + BLK, cols] = (acc * e / den * gate).astype(BF16)


def _sink_attention(qc, gc, kvc, sink_lanes, bias):
    bsz, seq, _ = qc.shape
    per = C_TILE // BLK
    return pl.pallas_call(
        _sink_kernel,
        out_shape=jax.ShapeDtypeStruct(qc.shape, BF16),
        grid=(bsz, seq // C_TILE),
        in_specs=[
            pl.BlockSpec((None, C_TILE, C_WIDTH), lambda b, i: (b, i, 0)),
            pl.BlockSpec((None, C_TILE, C_WIDTH), lambda b, i: (b, i, 0)),
            pl.BlockSpec((None, C_TILE, 2 * C_KV_WIDTH), lambda b, i: (b, i, 0)),
            pl.BlockSpec((None, BLK, 2 * C_KV_WIDTH), lambda b, i: (b, jnp.maximum(i * per - 1, 0), 0)),
            pl.BlockSpec(sink_lanes.shape, lambda b, i: (0, 0, 0)),
            pl.BlockSpec(bias.shape, lambda b, i: (0, 0, 0)),
        ],
        out_specs=pl.BlockSpec((None, C_TILE, C_WIDTH), lambda b, i: (b, i, 0)),
        compiler_params=pltpu.CompilerParams(dimension_semantics=("arbitrary", "arbitrary"),
                                             vmem_limit_bytes=VMEM_LIMIT),
        name="sink_attn",
    )(qc, gc, kvc, kvc, sink_lanes, bias)


SSD_TILE = 512


def _ssd_kernel(xc_ref, gz_ref, dt_ref, dtb_ref, alog_ref, dexp_ref, nw_ref,
                expand_ref, tri_ref, o_ref, state):
    q = SSD_CHUNK
    hg = B_HEADS // B_GROUPS
    gw = B_WIDTH // B_GROUPS

    @pl.when(pl.program_id(1) == 0)
    def _():
        state[...] = jnp.zeros_like(state)

    tri = tri_ref[...]
    causal = tri > 0
    expand = expand_ref[...]
    lo = lax.broadcasted_iota(jnp.int32, (q, LANES), 1) < HEAD_DIM
    a2 = -jnp.exp(alog_ref[...]) * LOG2E

    for c in range(SSD_TILE // q):
        t0 = c * q
        xs_b = xc_ref[t0:t0 + q, 0:B_WIDTH]
        xs = xs_b.astype(F32)
        dt = jax.nn.softplus(dt_ref[t0:t0 + q, :] + dtb_ref[...])
        da = dt * a2
        a_cum = _split_dot_left(tri, da)
        a_cum_t = a_cum.T
        dt_t = dt.T
        ea = jnp.exp2(a_cum)
        dte = jnp.exp2(a_cum[q - 1:q, :] - a_cum)
        ea_x = _split_dot(ea, expand)
        w_x = jnp.dot((dt * dte).astype(BF16), expand, preferred_element_type=F32)
        xw_b = (xs * w_x).astype(BF16)
        decay_x = ea_x[q - 1:q, :]

        y_parts = []
        for g in range(B_GROUPS):
            b0 = B_WIDTH + g * B_STATE
            c0 = B_WIDTH + (B_GROUPS + g) * B_STATE
            bm_b = xc_ref[t0:t0 + q, b0:b0 + B_STATE]
            cm_b = xc_ref[t0:t0 + q, c0:c0 + B_STATE]
            cbm = lax.dot_general(cm_b, bm_b, (((1,), (1,)), ((), ())), preferred_element_type=F32)
            st = state[:, g * gw:(g + 1) * gw]
            y_off = jnp.dot(cm_b, st.astype(BF16), preferred_element_type=F32) * ea_x[:, g * gw:(g + 1) * gw]
            bm_t = bm_b.astype(F32).T.astype(BF16)
            new = jnp.dot(bm_t, xw_b[:, g * gw:(g + 1) * gw], preferred_element_type=F32)
            state[:, g * gw:(g + 1) * gw] = st * decay_x[:, g * gw:(g + 1) * gw] + new
            y_diag = []
            for pr in range(hg // 2):
                h0 = g * hg + 2 * pr
                mats = []
                for h in (h0, h0 + 1):
                    seg = a_cum[:, h:h + 1] - a_cum_t[h:h + 1, :]
                    decay = jnp.exp2(jnp.where(causal, seg, NEG))
                    mats.append((cbm * decay * dt_t[h:h + 1, :]).astype(BF16))
                lhs = jnp.concatenate(mats, axis=0)
                x0 = (h0 // 2) * LANES
                yy = jnp.dot(lhs, xs_b[:, x0:x0 + LANES], preferred_element_type=F32)
                y_diag.append(jnp.where(lo, yy[:q], yy[q:]))
            y_parts.append(jnp.concatenate(y_diag, axis=1) + y_off)
        y = jnp.concatenate(y_parts, axis=1) + dexp_ref[...] * xs
        y = y * gz_ref[t0:t0 + q, :].astype(F32)
        outs = []
        for g in range(B_GROUPS):
            yg = y[:, g * gw:(g + 1) * gw]
            outs.append(yg * lax.rsqrt(jnp.mean(yg * yg, axis=-1, keepdims=True) + NORM_EPS))
        o_ref[t0:t0 + q, :] = (jnp.concatenate(outs, axis=1) * nw_ref[...]).astype(BF16)


def _split_dot_left(w_bf16, x):
    hi = x.astype(BF16)
    lo = (x - hi.astype(F32)).astype(BF16)
    return (jnp.dot(w_bf16, hi, preferred_element_type=F32)
            + jnp.dot(w_bf16, lo, preferred_element_type=F32))


def _ssd(xc, gzb, dt_raw, dt_bias_row, a_log_row, d_exp, norm_w, expand, tri):
    bsz, seq, _ = xc.shape
    const = lambda a: pl.BlockSpec(a.shape, lambda b, i: (0,) * a.ndim)
    return pl.pallas_call(
        _ssd_kernel,
        out_shape=jax.ShapeDtypeStruct((bsz, seq, B_WIDTH), BF16),
        grid=(bsz, seq // SSD_TILE),
        in_specs=[
            pl.BlockSpec((None, SSD_TILE, B_CONV_CH), lambda b, i: (b, i, 0)),
            pl.BlockSpec((None, SSD_TILE, B_WIDTH), lambda b, i: (b, i, 0)),
            pl.BlockSpec((None, SSD_TILE, DT_PAD), lambda b, i: (b, i, 0)),
            const(dt_bias_row), const(a_log_row), const(d_exp), const(norm_w),
            const(expand), const(tri),
        ],
        out_specs=pl.BlockSpec((None, SSD_TILE, B_WIDTH), lambda b, i: (b, i, 0)),
        scratch_shapes=[pltpu.VMEM((B_STATE, B_WIDTH), F32)],
        compiler_params=pltpu.CompilerParams(dimension_semantics=("arbitrary", "arbitrary"),
                                             vmem_limit_bytes=VMEM_LIMIT),
        name="ssd",
    )(xc, gzb, dt_raw, dt_bias_row, a_log_row, d_exp, norm_w, expand, tri)


def _outproj_kernel(ya_ref, yb_ref, yc_ref, x_ref, mod_ref, pw_ref, w_ref, permt_ref, o_ref):
    units = []
    for u in range(PERM_PER_TILE):
        ya_res = jnp.concatenate([ya_ref[r, u * BF16_ROWS:(u + 1) * BF16_ROWS, :] for r in range(RES)], axis=0)
        units.append(jnp.dot(permt_ref[...], ya_res, preferred_element_type=F32).astype(BF16))
    ya = jnp.concatenate(units, axis=0)
    y = jnp.dot(ya, w_ref[0:A_WIDTH, :], preferred_element_type=F32)
    y += jnp.dot(yb_ref[...], w_ref[A_WIDTH:A_WIDTH + B_WIDTH, :], preferred_element_type=F32)
    y += jnp.dot(yc_ref[...], w_ref[A_WIDTH + B_WIDTH:, :], preferred_element_type=F32)
    ms = jnp.mean(y * y, axis=-1, keepdims=True)
    yn = y * lax.rsqrt(ms + NORM_EPS) * pw_ref[...]
    o_ref[...] = x_ref[...] + mod_ref[:, 2 * D_MODEL:] * yn


def _outproj(ya, yb, yc, x, mod_i, post_w, w_out_packed, perm_t):
    bsz, seq, _ = x.shape
    nat_spec = lambda width: pl.BlockSpec((None, ROW_TILE, width), lambda b, i: (b, i, 0))
    return pl.pallas_call(
        _outproj_kernel,
        out_shape=jax.ShapeDtypeStruct(x.shape, F32),
        grid=(bsz, seq // ROW_TILE),
        in_specs=[
            pl.BlockSpec((None, RES, BF16_ROWS * PERM_PER_TILE, A_WIDTH), lambda b, i: (b, 0, i, 0)),
            nat_spec(B_WIDTH), nat_spec(C_WIDTH), nat_spec(D_MODEL),
            pl.BlockSpec((None, 1, 3 * D_MODEL), lambda b, i: (b, 0, 0)),
            pl.BlockSpec((1, D_MODEL), lambda b, i: (0, 0)),
            pl.BlockSpec(w_out_packed.shape, lambda b, i: (0, 0)),
            pl.BlockSpec((PERM_TILE, PERM_TILE), lambda b, i: (0, 0)),
        ],
        out_specs=nat_spec(D_MODEL),
        compiler_params=pltpu.CompilerParams(dimension_semantics=("arbitrary", "arbitrary"),
                                             vmem_limit_bytes=VMEM_LIMIT),
        name="out_proj",
    )(ya, yb, yc, x, mod_i, post_w, w_out_packed, perm_t)


def _c_head_perm():
    ngrp = C_WIDTH // LANES
    cols = []
    for g in range(ngrp):
        for half in range(2):
            h = g + ngrp * half
            cols.extend(range(h * HEAD_DIM, (h + 1) * HEAD_DIM))
    return np.asarray(cols)


def _pack_w_in(w):
    a_cols = 4 * A_WIDTH
    zb0 = a_cols
    xbc0 = zb0 + B_WIDTH
    dt0 = xbc0 + B_CONV_CH
    c0 = dt0 + B_HEADS
    perm = _c_head_perm()
    qc = w[:, c0:c0 + C_WIDTH][:, perm]
    zc = w[:, c0 + C_WIDTH:c0 + 2 * C_WIDTH][:, perm]
    kv = w[:, c0 + 2 * C_WIDTH:c0 + 2 * C_WIDTH + 2 * C_KV_WIDTH]
    dt = jnp.pad(w[:, dt0:dt0 + B_HEADS], ((0, 0), (0, DT_PAD - B_HEADS)))
    return jnp.concatenate([w[:, :dt0], qc, zc, kv, dt], axis=1).astype(BF16)


def _pack_w_out(w):
    perm = _c_head_perm()
    c0 = A_WIDTH + B_WIDTH
    return jnp.concatenate([w[:c0], w[c0:][perm]], axis=0).astype(BF16)


def kernel(x, c, ada_w, ada_b, pre_norm_w, post_norm_w, w_in, conv_w, conv_b, dt_bias, a_log, d_skip,
           ssm_norm_w, sinks, w_out):
    bsz, seq, _ = x.shape
    assert seq % (RES * BLK) == 0 and seq % SSD_TILE == 0 and seq % C_TILE == 0

    mod = _modulation(c, ada_w, ada_b)
    bias_first, bias_later = _dilated_biases()
    kpos = np.arange(2 * BLK) - BLK
    swa = _band_bias(np.arange(BLK), kpos, C_WINDOW)
    swa_first = np.where(kpos[None, :] >= 0, swa, NEG).astype(np.float32)
    bias_c = jnp.asarray(np.stack([swa_first, swa]))
    expand_np = np.zeros((LANES, B_WIDTH), np.float32)
    for h in range(B_HEADS):
        expand_np[h, h * HEAD_DIM:(h + 1) * HEAD_DIM] = 1.0
    expand = jnp.asarray(expand_np).astype(BF16)
    tri = jnp.asarray(np.tril(np.ones((SSD_CHUNK, SSD_CHUNK), np.float32))).astype(BF16)
    perm_np = _residue_perm()
    perm = jnp.asarray(perm_np).astype(BF16)
    perm_t = jnp.asarray(perm_np.T).astype(BF16)
    pad16 = lambda v: jnp.pad(v.astype(F32), (0, DT_PAD - B_HEADS)).reshape(1, DT_PAD)

    for i in range(DEPTH):
        qa, ka, va, ga, gzb, xc, dt_raw, qc, gc, kvc = _inproj(
            x, mod[i], pre_norm_w[i].reshape(1, D_MODEL), _pack_w_in(w_in[i]), perm,
            conv_w[i], conv_b[i].reshape(1, B_CONV_CH))

        ya = _dilated_attention(qa, ka, va, ga, bias_first, bias_later)

        yb = _ssd(xc, gzb, dt_raw, pad16(dt_bias[i]), pad16(a_log[i]),
                  jnp.repeat(d_skip[i].astype(F32), HEAD_DIM).reshape(1, B_WIDTH),
                  ssm_norm_w[i].reshape(1, B_WIDTH), expand, tri)

        sink_lanes = jnp.repeat((sinks[i].astype(F32) * LOG2E).reshape(2, C_HEADS // 2).T, HEAD_DIM, axis=1)
        yc = _sink_attention(qc, gc, kvc, sink_lanes.reshape(C_HEADS // 2, 1, LANES), bias_c)

        x = _outproj(ya, yb, yc, x, mod[i], post_norm_w[i].reshape(1, D_MODEL),
                     _pack_w_out(w_out[i]), perm_t)
    return x
```

```python
import functools
import math

import numpy as np
import jax
import jax.numpy as jnp
from jax import lax
from jax.experimental import pallas as pl
from jax.experimental.pallas import tpu as pltpu

F32 = jnp.float32
BF16 = jnp.bfloat16

D_MODEL = 1024
DEPTH = 4
HEAD_DIM = 64
LANES = 128
BF16_ROWS = 16
A_WIDTH = 512
B_WIDTH = 1024
B_HEADS = 16
B_GROUPS = 2
B_STATE = 128
B_CONV = 4
B_CONV_CH = B_WIDTH + 2 * B_GROUPS * B_STATE
SSD_CHUNK = 128
C_WIDTH = 512
C_HEADS = 8
C_KV_WIDTH = 128
C_WINDOW = 128
A_WINDOW = 128
NORM_EPS = 1e-6
RES = 16
BLK = 128
NEG = -1e30
LOG2E = math.log2(math.e)
DT_PAD = 128
IN_COLS_PACKED = 4 * A_WIDTH + B_WIDTH + B_CONV_CH + 2 * C_WIDTH + 2 * C_KV_WIDTH + DT_PAD
PERM_TILE = RES * BF16_ROWS
PERM_PER_TILE = 2
ROW_TILE = PERM_TILE * PERM_PER_TILE
SEG_COLS = 256
VMEM_LIMIT = 56 * 1024 * 1024
UNROLL = 8


def _silu(v):
    return v * jax.nn.sigmoid(v)


def _split_dot(x, w_bf16):
    hi = x.astype(BF16)
    lo = (x - hi.astype(F32)).astype(BF16)
    return (jnp.dot(hi, w_bf16, preferred_element_type=F32)
            + jnp.dot(lo, w_bf16, preferred_element_type=F32))


def _residue_perm():
    p = np.zeros((PERM_TILE, PERM_TILE), np.float32)
    for r in range(RES):
        for jj in range(BF16_ROWS):
            p[r * BF16_ROWS + jj, RES * jj + r] = 1.0
    return p


def _mod_kernel(c_ref, w_ref, b_ref, o_ref):
    ca = _silu(c_ref[...])
    o_ref[...] = jnp.dot(ca, w_ref[...], preferred_element_type=F32,
                         precision=lax.Precision.HIGHEST) + b_ref[...]


def _modulation(c, ada_w, ada_b):
    bsz = c.shape[0]
    ada_b4 = ada_b.reshape(DEPTH, 3, 1, D_MODEL)
    return pl.pallas_call(
        _mod_kernel,
        out_shape=jax.ShapeDtypeStruct((DEPTH, bsz, 3 * D_MODEL), F32),
        grid=(DEPTH, 3),
        in_specs=[
            pl.BlockSpec((bsz, D_MODEL), lambda i, k: (0, 0)),
            pl.BlockSpec((None, D_MODEL, D_MODEL), lambda i, k: (i, 0, k)),
            pl.BlockSpec((None, None, 1, D_MODEL), lambda i, k: (i, k, 0, 0)),
        ],
        out_specs=pl.BlockSpec((None, bsz, D_MODEL), lambda i, k: (i, 0, k)),
        compiler_params=pltpu.CompilerParams(dimension_semantics=("arbitrary", "arbitrary")),
        name="ada_mod",
    )(c, ada_w, ada_b4).reshape(DEPTH, bsz, 1, 3 * D_MODEL)


def _inproj_kernel(x_ref, mod_ref, pw_ref, w_ref, perm_ref, cw_ref, cb_ref,
                   qa_ref, ka_ref, va_ref, ga_ref, gb_ref, xc_ref, dt_ref,
                   qc_ref, gc_ref, kvc_ref, xpad, h, h_res):
    x = x_ref[...]
    ms = jnp.mean(x * x, axis=-1, keepdims=True)
    xn = x * lax.rsqrt(ms + NORM_EPS) * pw_ref[...]
    shift, scale_mod = mod_ref[:, 0:D_MODEL], mod_ref[:, D_MODEL:2 * D_MODEL]
    h[...] = (xn * (1.0 + scale_mod) + shift).astype(BF16)
    for u in range(PERM_PER_TILE):
        rows = slice(u * PERM_TILE, (u + 1) * PERM_TILE)
        h_res[rows, :] = jnp.dot(perm_ref[...], h[rows, :], preferred_element_type=F32).astype(BF16)

    def proj(lhs, c0, width):
        return jnp.dot(lhs[...], w_ref[:, c0:c0 + width], preferred_element_type=F32)

    def segments(width):
        return [(s, min(SEG_COLS, width - s)) for s in range(0, width, SEG_COLS)]

    def project_to(ref, lhs, c0, width, post):
        for s, w in segments(width):
            ref[:, s:s + w] = post(proj(lhs, c0 + s, w)).astype(ref.dtype)

    def project_grouped(ref, c0, post):
        for s, w in segments(A_WIDTH):
            val = post(proj(h_res, c0 + s, w)).astype(BF16)
            for u in range(PERM_PER_TILE):
                for r in range(RES):
                    row = u * PERM_TILE + r * BF16_ROWS
                    ref[r, u * BF16_ROWS:(u + 1) * BF16_ROWS, s:s + w] = val[row:row + BF16_ROWS]

    qscale = HEAD_DIM ** -0.5 * LOG2E
    off_zb = 4 * A_WIDTH
    off_xbc = off_zb + B_WIDTH
    off_qc = off_xbc + B_CONV_CH
    off_kv = off_qc + 2 * C_WIDTH
    off_dt = off_kv + 2 * C_KV_WIDTH
    pw = B_CONV_CH // 3

    @pl.when(pl.program_id(1) == 0)
    def _():
        xpad[0:CONV_PAD, :] = jnp.zeros((CONV_PAD, B_CONV_CH), F32)
    body = xpad.at[CONV_PAD:CONV_PAD + ROW_TILE, :]
    project_to(body, h, off_xbc, B_CONV_CH, lambda v: v)
    for part in range(3):
        cols = slice(part * pw, (part + 1) * pw)
        conv = cb_ref[:, cols] + sum(
            cw_ref[k:k + 1, cols] * xpad[CONV_PAD - (B_CONV - 1) + k:CONV_PAD - (B_CONV - 1) + k + ROW_TILE, cols]
            for k in range(B_CONV))
        xc_ref[:, cols] = _silu(conv).astype(BF16)
    xpad[0:CONV_PAD, :] = xpad[ROW_TILE:ROW_TILE + CONV_PAD, :]

    scaled = lambda v: v * qscale
    plain = lambda v: v
    project_grouped(ga_ref, 3 * A_WIDTH, _silu)
    project_to(gb_ref, h, off_zb, B_WIDTH, _silu)
    project_to(gc_ref, h, off_qc + C_WIDTH, C_WIDTH, _silu)
    project_grouped(qa_ref, 0, scaled)
    project_to(qc_ref, h, off_qc, C_WIDTH, scaled)
    project_grouped(ka_ref, A_WIDTH, plain)
    project_grouped(va_ref, 2 * A_WIDTH, plain)
    project_to(kvc_ref, h, off_kv, 2 * C_KV_WIDTH, plain)
    project_to(dt_ref, h, off_dt, DT_PAD, plain)


CONV_PAD = 8


def _inproj(x, mod_i, pre_w, w_packed, layer, perm, conv_w, conv_b):
    bsz, seq, _ = x.shape
    jn = seq // RES
    nat_spec = lambda width: pl.BlockSpec((None, ROW_TILE, width), lambda b, i: (b, i, 0))
    grp_spec = pl.BlockSpec((None, RES, BF16_ROWS * PERM_PER_TILE, A_WIDTH), lambda b, i: (b, 0, i, 0))
    nat = lambda width, dt: jax.ShapeDtypeStruct((bsz, seq, width), dt)
    grp = jax.ShapeDtypeStruct((bsz, RES, jn, A_WIDTH), BF16)
    return pl.pallas_call(
        _inproj_kernel,
        out_shape=[grp, grp, grp, grp,
                   nat(B_WIDTH, BF16), nat(B_CONV_CH, BF16), nat(DT_PAD, F32),
                   nat(C_WIDTH, BF16), nat(C_WIDTH, BF16), nat(2 * C_KV_WIDTH, BF16)],
        grid=(bsz, seq // ROW_TILE),
        in_specs=[
            nat_spec(D_MODEL),
            pl.BlockSpec((None, 1, 3 * D_MODEL), lambda b, i: (b, 0, 0)),
            pl.BlockSpec((1, D_MODEL), lambda b, i: (0, 0)),
            pl.BlockSpec((None, D_MODEL, IN_COLS_PACKED), lambda b, i: (layer, 0, 0),
                         pipeline_mode=pl.Buffered(1)),
            pl.BlockSpec((PERM_TILE, PERM_TILE), lambda b, i: (0, 0)),
            pl.BlockSpec(conv_w.shape, lambda b, i: (0, 0)),
            pl.BlockSpec(conv_b.shape, lambda b, i: (0, 0)),
        ],
        out_specs=[grp_spec, grp_spec, grp_spec, grp_spec,
                   nat_spec(B_WIDTH), nat_spec(B_CONV_CH), nat_spec(DT_PAD),
                   nat_spec(C_WIDTH), nat_spec(C_WIDTH), nat_spec(2 * C_KV_WIDTH)],
        scratch_shapes=[pltpu.VMEM((CONV_PAD + ROW_TILE, B_CONV_CH), F32),
                        pltpu.VMEM((ROW_TILE, D_MODEL), BF16), pltpu.VMEM((ROW_TILE, D_MODEL), BF16)],
        compiler_params=pltpu.CompilerParams(dimension_semantics=("arbitrary", "arbitrary"),
                                             vmem_limit_bytes=VMEM_LIMIT),
        name="in_proj",
    )(x, mod_i, pre_w, w_packed, perm, conv_w, conv_b)


def _head_masks():
    lane = lax.broadcasted_iota(jnp.int32, (BLK, LANES), 1)
    lo = lane < HEAD_DIM
    return lo, lo.astype(F32).astype(BF16), (~lo).astype(F32).astype(BF16)


def _attend_pairs(blocks, lo, mlo, mhi):
    scores = []
    for qb, kb, _, bias2 in blocks:
        qs = jnp.concatenate([qb * mlo, qb * mhi], axis=0)
        scores.append(lax.dot_general(qs, kb, (((1,), (1,)), ((), ())),
                                      preferred_element_type=F32) + bias2)
    outs = []
    for (_, _, vb, _), s in zip(blocks, scores):
        m = jnp.max(s, axis=-1, keepdims=True)
        p = jnp.exp2(s - m).astype(BF16)
        v_ext = jnp.concatenate([vb, jnp.ones_like(vb)], axis=1)
        pv = jnp.dot(p, v_ext, preferred_element_type=F32)
        outs.append((jnp.where(lo, m[:BLK], m[BLK:]),
                     jnp.where(lo, pv[:BLK, LANES:], pv[BLK:, LANES:]),
                     jnp.where(lo, pv[:BLK, :LANES], pv[BLK:, :LANES])))
    return outs


def _band_bias(qpos, kpos, window):
    dist = qpos[:, None] - kpos[None, :]
    ok = (dist >= 0) & (dist <= window)
    b = np.where(ok, 0.0, NEG).astype(np.float32)
    return np.concatenate([b, b], axis=0)


def _dilated_biases():
    first, later = [], []
    r, jj = np.meshgrid(np.arange(RES), np.arange(8), indexing="ij")
    qpos = (16 * jj + r).reshape(-1)
    r, jj = np.meshgrid(np.arange(RES), np.arange(16), indexing="ij")
    kpos = (16 * (jj - 8) + r).reshape(-1)
    first.append(_band_bias(qpos, qpos, A_WINDOW))
    later.append(_band_bias(qpos, kpos, A_WINDOW))
    rh, jj = np.meshgrid(np.arange(4), np.arange(32), indexing="ij")
    qpos = (4 * jj + rh).reshape(-1)
    rh, jj = np.meshgrid(np.arange(4), np.arange(64), indexing="ij")
    kpos = (4 * (jj - 32) + rh).reshape(-1)
    first.append(_band_bias(qpos, qpos, A_WINDOW))
    later.append(_band_bias(qpos, kpos, A_WINDOW))
    qpos = np.arange(BLK)
    kpos = np.arange(2 * BLK) - BLK
    first.append(_band_bias(qpos, qpos, A_WINDOW))
    later.append(_band_bias(qpos, kpos, A_WINDOW))
    return jnp.asarray(np.stack(first)), jnp.asarray(np.stack(later))


def _dilated_kernel(q_ref, k_ref, v_ref, g_ref, b1_ref, b2_ref, o_ref,
                    qf, kf, vf, m1, l1, a1, m2, l2, a2, m3, l3, a3):
    jn = q_ref.shape[1]
    lo, mlo, mhi = _head_masks()
    attend = functools.partial(_attend_pairs, lo=lo, mlo=mlo, mhi=mhi)

    def p3_blocks(r):
        blocks = [(q_ref[r, 0:BLK, :], k_ref[r, 0:BLK, :], v_ref[r, 0:BLK, :], b1_ref[2])]
        for jb in range(1, jn // BLK):
            s0 = (jb - 1) * BLK
            blocks.append((q_ref[r, s0 + BLK:s0 + 2 * BLK, :], k_ref[r, s0:s0 + 2 * BLK, :],
                           v_ref[r, s0:s0 + 2 * BLK, :], b2_ref[2]))
        return blocks

    def p3_store(r, outs):
        for jb, (mm, ll, acc) in enumerate(outs):
            m3[r, jb * BLK:(jb + 1) * BLK, :] = mm
            l3[r, jb * BLK:(jb + 1) * BLK, :] = ll
            a3[r, jb * BLK:(jb + 1) * BLK, :] = acc

    per3 = jn // BLK
    res3 = UNROLL // per3

    def p3_body(rr, carry):
        rs = [rr * res3 + d for d in range(res3)]
        outs = attend([blk for r in rs for blk in p3_blocks(r)])
        for d, r in enumerate(rs):
            p3_store(r, outs[d * per3:(d + 1) * per3])
        return carry
    lax.fori_loop(0, RES // res3, p3_body, 0)

    def p2_rows(ref, r4, start, size):
        return jnp.concatenate([ref[4 * rh + r4, pl.ds(start, size), :] for rh in range(4)], axis=0)

    def p2_store(r4, start, out):
        mm, ll, acc = out
        for rh in range(4):
            sl = slice(32 * rh, 32 * rh + 32)
            m2[4 * rh + r4, pl.ds(start, 32), :] = mm[sl]
            l2[4 * rh + r4, pl.ds(start, 32), :] = ll[sl]
            a2[4 * rh + r4, pl.ds(start, 32), :] = acc[sl]

    def p2_first(r4):
        return (p2_rows(q_ref, r4, 0, 32), p2_rows(k_ref, r4, 0, 32), p2_rows(v_ref, r4, 0, 32), b1_ref[1])

    def p2_later(r4, q0):
        k0 = q0 - 32 if isinstance(q0, int) else pl.multiple_of(q0 - 32, 32)
        return (p2_rows(q_ref, r4, q0, 32), p2_rows(k_ref, r4, k0, 64), p2_rows(v_ref, r4, k0, 64), b2_ref[1])

    steps2 = UNROLL // 4
    outs = attend([p2_first(r4) for r4 in range(4)]
                  + [p2_later(r4, 32 * b4) for b4 in range(1, steps2) for r4 in range(4)])
    for b4 in range(steps2):
        for r4 in range(4):
            p2_store(r4, 32 * b4, outs[4 * b4 + r4])

    def p2_body(i, carry):
        starts = [pl.multiple_of((i * steps2 + d) * 32, 32) for d in range(steps2)]
        outs = attend([p2_later(r4, q0) for q0 in starts for r4 in range(4)])
        for d, q0 in enumerate(starts):
            for r4 in range(4):
                p2_store(r4, q0, outs[4 * d + r4])
        return carry
    lax.fori_loop(1, jn // (32 * steps2), p2_body, 0)

    for r in range(RES):
        qf[r] = q_ref[r].astype(F32)
        kf[r] = k_ref[r].astype(F32)
        vf[r] = v_ref[r].astype(F32)

    def p1_rows(ref, start, size):
        return jnp.concatenate([ref[r, pl.ds(start, size), :] for r in range(RES)], axis=0).astype(BF16)

    def p1_store(start, out):
        mm, ll, acc = out
        for r in range(RES):
            sl = slice(8 * r, 8 * r + 8)
            m1[r, pl.ds(start, 8), :] = mm[sl]
            l1[r, pl.ds(start, 8), :] = ll[sl]
            a1[r, pl.ds(start, 8), :] = acc[sl]

    def p1_later(q0):
        k0 = q0 - 8 if isinstance(q0, int) else pl.multiple_of(q0 - 8, 8)
        return (p1_rows(qf, q0, 8), p1_rows(kf, k0, 16), p1_rows(vf, k0, 16), b2_ref[0])

    outs = attend([(p1_rows(qf, 0, 8), p1_rows(kf, 0, 8), p1_rows(vf, 0, 8), b1_ref[0])]
                  + [p1_later(8 * a) for a in range(1, UNROLL)])
    for a in range(UNROLL):
        p1_store(8 * a, outs[a])

    def p1_body(i, carry):
        starts = [pl.multiple_of((i * UNROLL + d) * 8, 8) for d in range(UNROLL)]
        outs = attend([p1_later(q0) for q0 in starts])
        for q0, out in zip(starts, outs):
            p1_store(q0, out)
        return carry
    lax.fori_loop(1, jn // (8 * UNROLL), p1_body, 0)

    def merge_body(r, carry):
        ma, mb, mc = m1[r], m2[r], m3[r]
        mx = jnp.maximum(jnp.maximum(ma, mb), mc)
        ea, eb, ec = jnp.exp2(ma - mx), jnp.exp2(mb - mx), jnp.exp2(mc - mx)
        num = ea * a1[r] + eb * a2[r] + ec * a3[r]
        den = ea * l1[r] + eb * l2[r] + ec * l3[r]
        o_ref[r] = (num / den * g_ref[r].astype(F32)).astype(BF16)
        return carry
    lax.fori_loop(0, RES, merge_body, 0)


def _dilated_attention(qa, ka, va, ga, bias_first, bias_later):
    bsz, _, jn, _ = qa.shape
    assert jn % (8 * UNROLL) == 0 and jn % BLK == 0
    spec = pl.BlockSpec((None, RES, jn, LANES), lambda b, hp: (b, 0, 0, hp))
    acc = pltpu.VMEM((RES, jn, LANES), F32)
    return pl.pallas_call(
        _dilated_kernel,
        out_shape=jax.ShapeDtypeStruct(qa.shape, BF16),
        grid=(bsz, A_WIDTH // LANES),
        in_specs=[spec, spec, spec, spec,
                  pl.BlockSpec(bias_first.shape, lambda b, hp: (0, 0, 0)),
                  pl.BlockSpec(bias_later.shape, lambda b, hp: (0, 0, 0))],
        out_specs=spec,
        scratch_shapes=[acc] * 12,
        compiler_params=pltpu.CompilerParams(dimension_semantics=("arbitrary", "arbitrary"),
                                             vmem_limit_bytes=VMEM_LIMIT),
        name="dilated_attn",
    )(qa, ka, va, ga, bias_first, bias_later)


C_TILE = 1024


def _sink_kernel(q_ref, g_ref, kv_ref, kvp_ref, sink_ref, bias_ref, o_ref):
    first_tile = pl.program_id(1) == 0
    lo, mlo, mhi = _head_masks()
    ngrp = C_WIDTH // LANES
    per_call = UNROLL // ngrp
    for u0 in range(0, C_TILE // BLK, per_call):
        blocks = []
        for u in range(u0, u0 + per_call):
            r0 = u * BLK
            cur = kv_ref[r0:r0 + BLK, :]
            if u == 0:
                prev = kvp_ref[...]
                bias = bias_ref[jnp.where(first_tile, 0, 1)]
            else:
                prev = kv_ref[r0 - BLK:r0, :]
                bias = bias_ref[1]
            kb = jnp.concatenate([prev[:, :LANES], cur[:, :LANES]], axis=0)
            vb = jnp.concatenate([prev[:, LANES:], cur[:, LANES:]], axis=0)
            for g in range(ngrp):
                blocks.append((q_ref[r0:r0 + BLK, g * LANES:(g + 1) * LANES], kb, vb, bias))
        outs = _attend_pairs(blocks, lo, mlo, mhi)
        for idx, (mm, ll, acc) in enumerate(outs):
            r0 = (u0 + idx // ngrp) * BLK
            cols = slice((idx % ngrp) * LANES, (idx % ngrp + 1) * LANES)
            sk = sink_ref[idx % ngrp]
            m2 = jnp.maximum(mm, sk)
            e = jnp.exp2(mm - m2)
            den = ll * e + jnp.exp2(sk - m2)
            gate = g_ref[r0:r0 + BLK, cols].astype(F32)
            o_ref[r0:r0 + BLK, cols] = (acc * e / den * gate).astype(BF16)


def _sink_attention(qc, gc, kvc, sink_lanes, bias):
    bsz, seq, _ = qc.shape
    per = C_TILE // BLK
    return pl.pallas_call(
        _sink_kernel,
        out_shape=jax.ShapeDtypeStruct(qc.shape, BF16),
        grid=(bsz, seq // C_TILE),
        in_specs=[
            pl.BlockSpec((None, C_TILE, C_WIDTH), lambda b, i: (b, i, 0)),
            pl.BlockSpec((None, C_TILE, C_WIDTH), lambda b, i: (b, i, 0)),
            pl.BlockSpec((None, C_TILE, 2 * C_KV_WIDTH), lambda b, i: (b, i, 0)),
            pl.BlockSpec((None, BLK, 2 * C_KV_WIDTH), lambda b, i: (b, jnp.maximum(i * per - 1, 0), 0)),
            pl.BlockSpec(sink_lanes.shape, lambda b, i: (0, 0, 0)),
            pl.BlockSpec(bias.shape, lambda b, i: (0, 0, 0)),
        ],
        out_specs=pl.BlockSpec((None, C_TILE, C_WIDTH), lambda b, i: (b, i, 0)),
        compiler_params=pltpu.CompilerParams(dimension_semantics=("arbitrary", "arbitrary"),
                                             vmem_limit_bytes=VMEM_LIMIT),
        name="sink_attn",
    )(qc, gc, kvc, kvc, sink_lanes, bias)


SSD_TILE = 1024
SSD_INTERLEAVE = 4


def _ssd_kernel(xc_ref, gz_ref, dt_ref, dtb_ref, alog_ref, dexp_ref, nw_ref,
                expand_ref, tri_ref, o_ref, state):
    q = SSD_CHUNK
    hg = B_HEADS // B_GROUPS
    gw = B_WIDTH // B_GROUPS

    @pl.when(pl.program_id(1) == 0)
    def _():
        state[...] = jnp.zeros_like(state)

    tri = tri_ref[...]
    causal = tri > 0
    expand = expand_ref[...]
    lo = lax.broadcasted_iota(jnp.int32, (q, LANES), 1) < HEAD_DIM
    a2 = -jnp.exp(alog_ref[...]) * LOG2E

    gcols = [slice(g * gw, (g + 1) * gw) for g in range(B_GROUPS)]
    st = [state[:, gc] for gc in gcols]

    for c_lo in range(0, SSD_TILE // q, SSD_INTERLEAVE):
        chunks = [dict(t0=c * q) for c in range(c_lo, c_lo + SSD_INTERLEAVE)]
        for ch in chunks:
            rows = slice(ch["t0"], ch["t0"] + q)
            ch["xs_b"] = xc_ref[rows, 0:B_WIDTH]
            dt = jax.nn.softplus(dt_ref[rows, :] + dtb_ref[...])
            da = dt * a2
            a_cum = _split_dot_left(tri, da)
            ch.update(dt=dt, a_cum=a_cum, a_cum_t=a_cum.T, dt_t=dt.T)
        for ch in chunks:
            a_cum = ch["a_cum"]
            dte = jnp.exp2(a_cum[q - 1:q, :] - a_cum)
            ch["ea_x"] = _split_dot(jnp.exp2(a_cum), expand)
            w_x = jnp.dot((ch["dt"] * dte).astype(BF16), expand, preferred_element_type=F32)
            ch["xw_b"] = (ch["xs_b"].astype(F32) * w_x).astype(BF16)
        for ch in chunks:
            rows = slice(ch["t0"], ch["t0"] + q)
            ch["cm_b"], ch["cbm"], ch["new"] = [], [], []
            for g in range(B_GROUPS):
                b0 = B_WIDTH + g * B_STATE
                c0 = B_WIDTH + (B_GROUPS + g) * B_STATE
                bm_b = xc_ref[rows, b0:b0 + B_STATE]
                cm_b = xc_ref[rows, c0:c0 + B_STATE]
                ch["cm_b"].append(cm_b)
                ch["cbm"].append(lax.dot_general(cm_b, bm_b, (((1,), (1,)), ((), ())),
                                                 preferred_element_type=F32))
                bm_t = bm_b.astype(F32).T.astype(BF16)
                ch["new"].append(jnp.dot(bm_t, ch["xw_b"][:, gcols[g]], preferred_element_type=F32))
        for ch in chunks:
            ch["y_off"] = []
            for g in range(B_GROUPS):
                ea_g = ch["ea_x"][:, gcols[g]]
                ch["y_off"].append(jnp.dot(ch["cm_b"][g], st[g].astype(BF16), preferred_element_type=F32) * ea_g)
                st[g] = st[g] * ea_g[q - 1:q, :] + ch["new"][g]
        for ch in chunks:
            y_parts = []
            for g in range(B_GROUPS):
                y_diag = []
                for pr in range(hg // 2):
                    h0 = g * hg + 2 * pr
                    mats = []
                    for h in (h0, h0 + 1):
                        seg = ch["a_cum"][:, h:h + 1] - ch["a_cum_t"][h:h + 1, :]
                        decay = jnp.exp2(jnp.where(causal, seg, NEG))
                        mats.append((ch["cbm"][g] * decay * ch["dt_t"][h:h + 1, :]).astype(BF16))
                    lhs = jnp.concatenate(mats, axis=0)
                    x0 = (h0 // 2) * LANES
                    yy = jnp.dot(lhs, ch["xs_b"][:, x0:x0 + LANES], preferred_element_type=F32)
                    y_diag.append(jnp.where(lo, yy[:q], yy[q:]))
                y_parts.append(jnp.concatenate(y_diag, axis=1) + ch["y_off"][g])
            ch["y"] = jnp.concatenate(y_parts, axis=1)
        for ch in chunks:
            rows = slice(ch["t0"], ch["t0"] + q)
            y = ch["y"] + dexp_ref[...] * ch["xs_b"].astype(F32)
            y = y * gz_ref[rows, :].astype(F32)
            outs = []
            for g in range(B_GROUPS):
                yg = y[:, gcols[g]]
                outs.append(yg * lax.rsqrt(jnp.mean(yg * yg, axis=-1, keepdims=True) + NORM_EPS))
            o_ref[rows, :] = (jnp.concatenate(outs, axis=1) * nw_ref[...]).astype(BF16)

    for g in range(B_GROUPS):
        state[:, gcols[g]] = st[g]


def _split_dot_left(w_bf16, x):
    hi = x.astype(BF16)
    lo = (x - hi.astype(F32)).astype(BF16)
    return (jnp.dot(w_bf16, hi, preferred_element_type=F32)
            + jnp.dot(w_bf16, lo, preferred_element_type=F32))


def _ssd(xc, gzb, dt_raw, dt_bias_row, a_log_row, d_exp, norm_w, expand, tri):
    bsz, seq, _ = xc.shape
    const = lambda a: pl.BlockSpec(a.shape, lambda b, i: (0,) * a.ndim)
    return pl.pallas_call(
        _ssd_kernel,
        out_shape=jax.ShapeDtypeStruct((bsz, seq, B_WIDTH), BF16),
        grid=(bsz, seq // SSD_TILE),
        in_specs=[
            pl.BlockSpec((None, SSD_TILE, B_CONV_CH), lambda b, i: (b, i, 0)),
            pl.BlockSpec((None, SSD_TILE, B_WIDTH), lambda b, i: (b, i, 0)),
            pl.BlockSpec((None, SSD_TILE, DT_PAD), lambda b, i: (b, i, 0)),
            const(dt_bias_row), const(a_log_row), const(d_exp), const(norm_w),
            const(expand), const(tri),
        ],
        out_specs=pl.BlockSpec((None, SSD_TILE, B_WIDTH), lambda b, i: (b, i, 0)),
        scratch_shapes=[pltpu.VMEM((B_STATE, B_WIDTH), F32)],
        compiler_params=pltpu.CompilerParams(dimension_semantics=("arbitrary", "arbitrary"),
                                             vmem_limit_bytes=VMEM_LIMIT),
        name="ssd",
    )(xc, gzb, dt_raw, dt_bias_row, a_log_row, d_exp, norm_w, expand, tri)


def _outproj_kernel(ya_ref, yb_ref, yc_ref, x_ref, mod_ref, pw_ref, w_ref, permt_ref, o_ref):
    units = []
    for u in range(PERM_PER_TILE):
        ya_res = jnp.concatenate([ya_ref[r, u * BF16_ROWS:(u + 1) * BF16_ROWS, :] for r in range(RES)], axis=0)
        units.append(jnp.dot(permt_ref[...], ya_res, preferred_element_type=F32).astype(BF16))
    ya = jnp.concatenate(units, axis=0)
    ycat = jnp.concatenate([ya, yb_ref[...], yc_ref[...]], axis=1)
    y = jnp.dot(ycat, w_ref[...], preferred_element_type=F32)
    ms = jnp.mean(y * y, axis=-1, keepdims=True)
    yn = y * lax.rsqrt(ms + NORM_EPS) * pw_ref[...]
    o_ref[...] = x_ref[...] + mod_ref[:, 2 * D_MODEL:] * yn


def _outproj(ya, yb, yc, x, mod_i, post_w, w_out_packed, layer, perm_t):
    bsz, seq, _ = x.shape
    nat_spec = lambda width: pl.BlockSpec((None, ROW_TILE, width), lambda b, i: (b, i, 0))
    return pl.pallas_call(
        _outproj_kernel,
        out_shape=jax.ShapeDtypeStruct(x.shape, F32),
        grid=(bsz, seq // ROW_TILE),
        in_specs=[
            pl.BlockSpec((None, RES, BF16_ROWS * PERM_PER_TILE, A_WIDTH), lambda b, i: (b, 0, i, 0)),
            nat_spec(B_WIDTH), nat_spec(C_WIDTH), nat_spec(D_MODEL),
            pl.BlockSpec((None, 1, 3 * D_MODEL), lambda b, i: (b, 0, 0)),
            pl.BlockSpec((1, D_MODEL), lambda b, i: (0, 0)),
            pl.BlockSpec((None,) + w_out_packed.shape[1:], lambda b, i: (layer, 0, 0)),
            pl.BlockSpec((PERM_TILE, PERM_TILE), lambda b, i: (0, 0)),
        ],
        out_specs=nat_spec(D_MODEL),
        compiler_params=pltpu.CompilerParams(dimension_semantics=("arbitrary", "arbitrary"),
                                             vmem_limit_bytes=VMEM_LIMIT),
        name="out_proj",
    )(ya, yb, yc, x, mod_i, post_w, w_out_packed, perm_t)


def _c_head_order():
    ngrp = C_WIDTH // LANES
    return [g + ngrp * half for g in range(ngrp) for half in range(2)]


def _reorder_c_heads(w, axis):
    heads = [lax.slice_in_dim(w, h * HEAD_DIM, (h + 1) * HEAD_DIM, axis=axis) for h in _c_head_order()]
    return jnp.concatenate(heads, axis=axis)


def _pack_w_in(w):
    dt0 = 4 * A_WIDTH + B_WIDTH + B_CONV_CH
    c0 = dt0 + B_HEADS
    w = w.astype(BF16)
    qc = _reorder_c_heads(w[:, :, c0:c0 + C_WIDTH], 2)
    zc = _reorder_c_heads(w[:, :, c0 + C_WIDTH:c0 + 2 * C_WIDTH], 2)
    kv = w[:, :, c0 + 2 * C_WIDTH:c0 + 2 * C_WIDTH + 2 * C_KV_WIDTH]
    dt = jnp.pad(w[:, :, dt0:dt0 + B_HEADS], ((0, 0), (0, 0), (0, DT_PAD - B_HEADS)))
    return jnp.concatenate([w[:, :, :dt0], qc, zc, kv, dt], axis=2)


def _pack_w_out(w):
    c0 = A_WIDTH + B_WIDTH
    w = w.astype(BF16)
    return jnp.concatenate([w[:, :c0], _reorder_c_heads(w[:, c0:], 1)], axis=1)


def kernel(x, c, ada_w, ada_b, pre_norm_w, post_norm_w, w_in, conv_w, conv_b, dt_bias, a_log, d_skip,
           ssm_norm_w, sinks, w_out):
    bsz, seq, _ = x.shape
    assert seq % (RES * BLK) == 0 and seq % SSD_TILE == 0 and seq % C_TILE == 0

    mod = _modulation(c, ada_w, ada_b)
    bias_first, bias_later = _dilated_biases()
    kpos = np.arange(2 * BLK) - BLK
    swa = _band_bias(np.arange(BLK), kpos, C_WINDOW)
    swa_first = np.where(kpos[None, :] >= 0, swa, NEG).astype(np.float32)
    bias_c = jnp.asarray(np.stack([swa_first, swa]))
    expand_np = np.zeros((LANES, B_WIDTH), np.float32)
    for h in range(B_HEADS):
        expand_np[h, h * HEAD_DIM:(h + 1) * HEAD_DIM] = 1.0
    expand = jnp.asarray(expand_np).astype(BF16)
    tri = jnp.asarray(np.tril(np.ones((SSD_CHUNK, SSD_CHUNK), np.float32))).astype(BF16)
    perm_np = _residue_perm()
    perm = jnp.asarray(perm_np).astype(BF16)
    perm_t = jnp.asarray(perm_np.T).astype(BF16)
    pad16 = lambda v: jnp.pad(v.astype(F32), (0, DT_PAD - B_HEADS)).reshape(1, DT_PAD)

    w_in_packed = _pack_w_in(w_in)
    w_out_packed = _pack_w_out(w_out)
    for i in range(DEPTH):
        qa, ka, va, ga, gzb, xc, dt_raw, qc, gc, kvc = _inproj(
            x, mod[i], pre_norm_w[i].reshape(1, D_MODEL), w_in_packed, i, perm,
            conv_w[i], conv_b[i].reshape(1, B_CONV_CH))

        ya = _dilated_attention(qa, ka, va, ga, bias_first, bias_later)

        yb = _ssd(xc, gzb, dt_raw, pad16(dt_bias[i]), pad16(a_log[i]),
                  jnp.repeat(d_skip[i].astype(F32), HEAD_DIM).reshape(1, B_WIDTH),
                  ssm_norm_w[i].reshape(1, B_WIDTH), expand, tri)

        sink_lanes = jnp.repeat((sinks[i].astype(F32) * LOG2E).reshape(2, C_HEADS // 2).T, HEAD_DIM, axis=1)
        yc = _sink_attention(qc, gc, kvc, sink_lanes.reshape(C_HEADS // 2, 1, LANES), bias_c)

        x = _outproj(ya, yb, yc, x, mod[i], post_norm_w[i].reshape(1, D_MODEL),
                     w_out_packed, i, perm_t)
    return x
```

```python
import functools
import math

import numpy as np
import jax
import jax.numpy as jnp
from jax import lax
from jax.experimental import pallas as pl
from jax.experimental.pallas import tpu as pltpu

F32 = jnp.float32
BF16 = jnp.bfloat16

D_MODEL = 1024
DEPTH = 4
HEAD_DIM = 64
LANES = 128
BF16_ROWS = 16
A_WIDTH = 512
B_WIDTH = 1024
B_HEADS = 16
B_GROUPS = 2
B_STATE = 128
B_CONV = 4
B_CONV_CH = B_WIDTH + 2 * B_GROUPS * B_STATE
SSD_CHUNK = 128
C_WIDTH = 512
C_HEADS = 8
C_KV_WIDTH = 128
C_WINDOW = 128
A_WINDOW = 128
NORM_EPS = 1e-6
RES = 16
BLK = 128
NEG = -1e30
LOG2E = math.log2(math.e)
DT_PAD = 128
IN_COLS_MAIN = 4 * A_WIDTH + B_WIDTH + B_CONV_CH
IN_COLS_TAIL = 2 * C_WIDTH + 2 * C_KV_WIDTH + DT_PAD
PERM_TILE = RES * BF16_ROWS
PERM_PER_TILE = 2
ROW_TILE = PERM_TILE * PERM_PER_TILE
SEG_COLS = 256
VMEM_LIMIT = 56 * 1024 * 1024
UNROLL = 16


def _silu(v):
    return v * jax.nn.sigmoid(v)


def _split_dot(x, w_bf16):
    hi = x.astype(BF16)
    lo = (x - hi.astype(F32)).astype(BF16)
    return (jnp.dot(hi, w_bf16, preferred_element_type=F32)
            + jnp.dot(lo, w_bf16, preferred_element_type=F32))


def _residue_perm():
    p = np.zeros((PERM_TILE, PERM_TILE), np.float32)
    for r in range(RES):
        for jj in range(BF16_ROWS):
            p[r * BF16_ROWS + jj, RES * jj + r] = 1.0
    return p


def _mod_kernel(c_ref, w_ref, b_ref, o_ref):
    ca = _silu(c_ref[...])
    o_ref[...] = jnp.dot(ca, w_ref[...], preferred_element_type=F32,
                         precision=lax.Precision.HIGHEST) + b_ref[...]


def _modulation(c, ada_w, ada_b):
    bsz = c.shape[0]
    ada_b4 = ada_b.reshape(DEPTH, 3, 1, D_MODEL)
    return pl.pallas_call(
        _mod_kernel,
        out_shape=jax.ShapeDtypeStruct((DEPTH, bsz, 3 * D_MODEL), F32),
        grid=(DEPTH, 3),
        in_specs=[
            pl.BlockSpec((bsz, D_MODEL), lambda i, k: (0, 0)),
            pl.BlockSpec((None, D_MODEL, D_MODEL), lambda i, k: (i, 0, k)),
            pl.BlockSpec((None, None, 1, D_MODEL), lambda i, k: (i, k, 0, 0)),
        ],
        out_specs=pl.BlockSpec((None, bsz, D_MODEL), lambda i, k: (i, 0, k)),
        compiler_params=pltpu.CompilerParams(dimension_semantics=("arbitrary", "arbitrary")),
        name="ada_mod",
    )(c, ada_w, ada_b4).reshape(DEPTH, bsz, 1, 3 * D_MODEL)


def _inproj_kernel(x_ref, mod_ref, pw_ref, w_ref, wt_ref, perm_ref, cw_ref, cb_ref,
                   qa_ref, ka_ref, va_ref, ga_ref, gb_ref, xc_ref, dt_ref,
                   qc_ref, gc_ref, kvc_ref, xpad, h, h_res):
    x = x_ref[...]
    ms = jnp.mean(x * x, axis=-1, keepdims=True)
    xn = x * lax.rsqrt(ms + NORM_EPS) * pw_ref[...]
    shift, scale_mod = mod_ref[:, 0:D_MODEL], mod_ref[:, D_MODEL:2 * D_MODEL]
    h[...] = (xn * (1.0 + scale_mod) + shift).astype(BF16)
    for u in range(PERM_PER_TILE):
        rows = slice(u * PERM_TILE, (u + 1) * PERM_TILE)
        h_res[rows, :] = jnp.dot(perm_ref[...], h[rows, :], preferred_element_type=F32).astype(BF16)

    def proj(lhs, c0, width):
        if c0 >= IN_COLS_MAIN:
            rhs = wt_ref[:, c0 - IN_COLS_MAIN:c0 - IN_COLS_MAIN + width]
        else:
            rhs = w_ref[:, c0:c0 + width]
        return jnp.dot(lhs[...], rhs, preferred_element_type=F32)

    def segments(width):
        return [(s, min(SEG_COLS, width - s)) for s in range(0, width, SEG_COLS)]

    def project_to(ref, lhs, c0, width, post):
        for s, w in segments(width):
            ref[:, s:s + w] = post(proj(lhs, c0 + s, w)).astype(ref.dtype)

    def project_grouped(ref, c0, post):
        for s, w in segments(A_WIDTH):
            val = post(proj(h_res, c0 + s, w)).astype(BF16)
            for u in range(PERM_PER_TILE):
                for r in range(RES):
                    row = u * PERM_TILE + r * BF16_ROWS
                    ref[r, u * BF16_ROWS:(u + 1) * BF16_ROWS, s:s + w] = val[row:row + BF16_ROWS]

    qscale = HEAD_DIM ** -0.5 * LOG2E
    off_zb = 4 * A_WIDTH
    off_xbc = off_zb + B_WIDTH
    off_qc = off_xbc + B_CONV_CH
    off_kv = off_qc + 2 * C_WIDTH
    off_dt = off_kv + 2 * C_KV_WIDTH
    pw = B_CONV_CH // 3

    @pl.when(pl.program_id(1) == 0)
    def _():
        xpad[0:CONV_PAD, :] = jnp.zeros((CONV_PAD, B_CONV_CH), F32)
    body = xpad.at[CONV_PAD:CONV_PAD + ROW_TILE, :]
    project_to(body, h, off_xbc, B_CONV_CH, lambda v: v)
    for part in range(3):
        cols = slice(part * pw, (part + 1) * pw)
        conv = cb_ref[:, cols] + sum(
            cw_ref[k:k + 1, cols] * xpad[CONV_PAD - (B_CONV - 1) + k:CONV_PAD - (B_CONV - 1) + k + ROW_TILE, cols]
            for k in range(B_CONV))
        xc_ref[:, cols] = _silu(conv).astype(BF16)
    xpad[0:CONV_PAD, :] = xpad[ROW_TILE:ROW_TILE + CONV_PAD, :]

    scaled = lambda v: v * qscale
    plain = lambda v: v
    project_grouped(ga_ref, 3 * A_WIDTH, _silu)
    project_to(gb_ref, h, off_zb, B_WIDTH, _silu)
    project_to(gc_ref, h, off_qc + C_WIDTH, C_WIDTH, _silu)
    project_grouped(qa_ref, 0, scaled)
    project_to(qc_ref, h, off_qc, C_WIDTH, scaled)
    project_grouped(ka_ref, A_WIDTH, plain)
    project_grouped(va_ref, 2 * A_WIDTH, plain)
    project_to(kvc_ref, h, off_kv, 2 * C_KV_WIDTH, plain)
    project_to(dt_ref, h, off_dt, DT_PAD, plain)


CONV_PAD = 8


def _inproj(x, mod_i, pre_w, w_main, w_tail, layer, perm, conv_w, conv_b):
    bsz, seq, _ = x.shape
    jn = seq // RES
    nat_spec = lambda width: pl.BlockSpec((None, ROW_TILE, width), lambda b, i: (b, i, 0))
    grp_spec = pl.BlockSpec((None, RES, BF16_ROWS * PERM_PER_TILE, A_WIDTH), lambda b, i: (b, 0, i, 0))
    nat = lambda width, dt: jax.ShapeDtypeStruct((bsz, seq, width), dt)
    grp = jax.ShapeDtypeStruct((bsz, RES, jn, A_WIDTH), BF16)
    return pl.pallas_call(
        _inproj_kernel,
        out_shape=[grp, grp, grp, grp,
                   nat(B_WIDTH, BF16), nat(B_CONV_CH, BF16), nat(DT_PAD, F32),
                   nat(C_WIDTH, BF16), nat(C_WIDTH, BF16), nat(2 * C_KV_WIDTH, BF16)],
        grid=(bsz, seq // ROW_TILE),
        in_specs=[
            nat_spec(D_MODEL),
            pl.BlockSpec((None, 1, 3 * D_MODEL), lambda b, i: (b, 0, 0)),
            pl.BlockSpec((1, D_MODEL), lambda b, i: (0, 0)),
            pl.BlockSpec((None, D_MODEL, IN_COLS_MAIN), lambda b, i: (layer, 0, 0),
                         pipeline_mode=pl.Buffered(1)),
            pl.BlockSpec((None, D_MODEL, IN_COLS_TAIL), lambda b, i: (layer, 0, 0),
                         pipeline_mode=pl.Buffered(1)),
            pl.BlockSpec((PERM_TILE, PERM_TILE), lambda b, i: (0, 0)),
            pl.BlockSpec(conv_w.shape, lambda b, i: (0, 0)),
            pl.BlockSpec(conv_b.shape, lambda b, i: (0, 0)),
        ],
        out_specs=[grp_spec, grp_spec, grp_spec, grp_spec,
                   nat_spec(B_WIDTH), nat_spec(B_CONV_CH), nat_spec(DT_PAD),
                   nat_spec(C_WIDTH), nat_spec(C_WIDTH), nat_spec(2 * C_KV_WIDTH)],
        scratch_shapes=[pltpu.VMEM((CONV_PAD + ROW_TILE, B_CONV_CH), F32),
                        pltpu.VMEM((ROW_TILE, D_MODEL), BF16), pltpu.VMEM((ROW_TILE, D_MODEL), BF16)],
        compiler_params=pltpu.CompilerParams(dimension_semantics=("arbitrary", "arbitrary"),
                                             vmem_limit_bytes=VMEM_LIMIT),
        name="in_proj",
    )(x, mod_i, pre_w, w_main, w_tail, perm, conv_w, conv_b)


def _head_masks():
    lane = lax.broadcasted_iota(jnp.int32, (BLK, LANES), 1)
    lo = lane < HEAD_DIM
    return lo, lo.astype(F32).astype(BF16), (~lo).astype(F32).astype(BF16)


def _attend_pairs(blocks, lo, mlo, mhi):
    scores = []
    for qb, kb, _, bias2 in blocks:
        qs = jnp.concatenate([qb * mlo, qb * mhi], axis=0)
        scores.append(lax.dot_general(qs, kb, (((1,), (1,)), ((), ())),
                                      preferred_element_type=F32) + bias2)
    outs = []
    for (_, _, vb, _), s in zip(blocks, scores):
        m = jnp.max(s, axis=-1, keepdims=True)
        p = jnp.exp2(s - m).astype(BF16)
        v_ext = jnp.concatenate([vb, jnp.ones_like(vb)], axis=1)
        pv = jnp.dot(p, v_ext, preferred_element_type=F32)
        outs.append((jnp.where(lo, m[:BLK], m[BLK:]),
                     jnp.where(lo, pv[:BLK, LANES:], pv[BLK:, LANES:]),
                     jnp.where(lo, pv[:BLK, :LANES], pv[BLK:, :LANES])))
    return outs


def _band_bias(qpos, kpos, window):
    dist = qpos[:, None] - kpos[None, :]
    ok = (dist >= 0) & (dist <= window)
    b = np.where(ok, 0.0, NEG).astype(np.float32)
    return np.concatenate([b, b], axis=0)


def _dilated_biases():
    first, later = [], []
    r, jj = np.meshgrid(np.arange(RES), np.arange(8), indexing="ij")
    qpos = (16 * jj + r).reshape(-1)
    r, jj = np.meshgrid(np.arange(RES), np.arange(16), indexing="ij")
    kpos = (16 * (jj - 8) + r).reshape(-1)
    first.append(_band_bias(qpos, qpos, A_WINDOW))
    later.append(_band_bias(qpos, kpos, A_WINDOW))
    rh, jj = np.meshgrid(np.arange(4), np.arange(32), indexing="ij")
    qpos = (4 * jj + rh).reshape(-1)
    rh, jj = np.meshgrid(np.arange(4), np.arange(64), indexing="ij")
    kpos = (4 * (jj - 32) + rh).reshape(-1)
    first.append(_band_bias(qpos, qpos, A_WINDOW))
    later.append(_band_bias(qpos, kpos, A_WINDOW))
    qpos = np.arange(BLK)
    kpos = np.arange(2 * BLK) - BLK
    first.append(_band_bias(qpos, qpos, A_WINDOW))
    later.append(_band_bias(qpos, kpos, A_WINDOW))
    return jnp.asarray(np.stack(first)), jnp.asarray(np.stack(later))


def _dilated_kernel(q_ref, k_ref, v_ref, g_ref, b1_ref, b2_ref, o_ref,
                    qf, kf, vf, m1, l1, a1, m2, l2, a2, m3, l3, a3):
    jn = q_ref.shape[1]
    lo, mlo, mhi = _head_masks()
    attend = functools.partial(_attend_pairs, lo=lo, mlo=mlo, mhi=mhi)

    def p3_blocks(r):
        blocks = [(q_ref[r, 0:BLK, :], k_ref[r, 0:BLK, :], v_ref[r, 0:BLK, :], b1_ref[2])]
        for jb in range(1, jn // BLK):
            s0 = (jb - 1) * BLK
            blocks.append((q_ref[r, s0 + BLK:s0 + 2 * BLK, :], k_ref[r, s0:s0 + 2 * BLK, :],
                           v_ref[r, s0:s0 + 2 * BLK, :], b2_ref[2]))
        return blocks

    def p3_store(r, outs):
        for jb, (mm, ll, acc) in enumerate(outs):
            m3[r, jb * BLK:(jb + 1) * BLK, :] = mm
            l3[r, jb * BLK:(jb + 1) * BLK, :] = ll
            a3[r, jb * BLK:(jb + 1) * BLK, :] = acc

    per3 = jn // BLK
    res3 = UNROLL // per3

    def p3_body(rr, carry):
        rs = [rr * res3 + d for d in range(res3)]
        outs = attend([blk for r in rs for blk in p3_blocks(r)])
        for d, r in enumerate(rs):
            p3_store(r, outs[d * per3:(d + 1) * per3])
        return carry
    lax.fori_loop(0, RES // res3, p3_body, 0)

    def p2_rows(ref, r4, start, size):
        return jnp.concatenate([ref[4 * rh + r4, pl.ds(start, size), :] for rh in range(4)], axis=0)

    def p2_store(r4, start, out):
        mm, ll, acc = out
        for rh in range(4):
            sl = slice(32 * rh, 32 * rh + 32)
            m2[4 * rh + r4, pl.ds(start, 32), :] = mm[sl]
            l2[4 * rh + r4, pl.ds(start, 32), :] = ll[sl]
            a2[4 * rh + r4, pl.ds(start, 32), :] = acc[sl]

    def p2_first(r4):
        return (p2_rows(q_ref, r4, 0, 32), p2_rows(k_ref, r4, 0, 32), p2_rows(v_ref, r4, 0, 32), b1_ref[1])

    def p2_later(r4, q0):
        k0 = q0 - 32 if isinstance(q0, int) else pl.multiple_of(q0 - 32, 32)
        return (p2_rows(q_ref, r4, q0, 32), p2_rows(k_ref, r4, k0, 64), p2_rows(v_ref, r4, k0, 64), b2_ref[1])

    steps2 = UNROLL // 4
    outs = attend([p2_first(r4) for r4 in range(4)]
                  + [p2_later(r4, 32 * b4) for b4 in range(1, steps2) for r4 in range(4)])
    for b4 in range(steps2):
        for r4 in range(4):
            p2_store(r4, 32 * b4, outs[4 * b4 + r4])

    def p2_body(i, carry):
        starts = [pl.multiple_of((i * steps2 + d) * 32, 32) for d in range(steps2)]
        outs = attend([p2_later(r4, q0) for q0 in starts for r4 in range(4)])
        for d, q0 in enumerate(starts):
            for r4 in range(4):
                p2_store(r4, q0, outs[4 * d + r4])
        return carry
    lax.fori_loop(1, jn // (32 * steps2), p2_body, 0)

    for r in range(RES):
        qf[r] = q_ref[r].astype(F32)
        kf[r] = k_ref[r].astype(F32)
        vf[r] = v_ref[r].astype(F32)

    def p1_rows(ref, start, size):
        return jnp.concatenate([ref[r, pl.ds(start, size), :] for r in range(RES)], axis=0).astype(BF16)

    def p1_store(start, out):
        mm, ll, acc = out
        for r in range(RES):
            sl = slice(8 * r, 8 * r + 8)
            m1[r, pl.ds(start, 8), :] = mm[sl]
            l1[r, pl.ds(start, 8), :] = ll[sl]
            a1[r, pl.ds(start, 8), :] = acc[sl]

    def p1_later(q0):
        k0 = q0 - 8 if isinstance(q0, int) else pl.multiple_of(q0 - 8, 8)
        return (p1_rows(qf, q0, 8), p1_rows(kf, k0, 16), p1_rows(vf, k0, 16), b2_ref[0])

    outs = attend([(p1_rows(qf, 0, 8), p1_rows(kf, 0, 8), p1_rows(vf, 0, 8), b1_ref[0])]
                  + [p1_later(8 * a) for a in range(1, UNROLL)])
    for a in range(UNROLL):
        p1_store(8 * a, outs[a])

    def p1_body(i, carry):
        starts = [pl.multiple_of((i * UNROLL + d) * 8, 8) for d in range(UNROLL)]
        outs = attend([p1_later(q0) for q0 in starts])
        for q0, out in zip(starts, outs):
            p1_store(q0, out)
        return carry
    lax.fori_loop(1, jn // (8 * UNROLL), p1_body, 0)

    def merge_body(r, carry):
        ma, mb, mc = m1[r], m2[r], m3[r]
        mx = jnp.maximum(jnp.maximum(ma, mb), mc)
        ea, eb, ec = jnp.exp2(ma - mx), jnp.exp2(mb - mx), jnp.exp2(mc - mx)
        num = ea * a1[r] + eb * a2[r] + ec * a3[r]
        den = ea * l1[r] + eb * l2[r] + ec * l3[r]
        o_ref[r] = (num / den * g_ref[r].astype(F32)).astype(BF16)
        return carry
    lax.fori_loop(0, RES, merge_body, 0)


def _dilated_attention(qa, ka, va, ga, bias_first, bias_later):
    bsz, _, jn, _ = qa.shape
    assert jn % (8 * UNROLL) == 0 and jn % BLK == 0
    spec = pl.BlockSpec((None, RES, jn, LANES), lambda b, hp: (b, 0, 0, hp))
    acc = pltpu.VMEM((RES, jn, LANES), F32)
    return pl.pallas_call(
        _dilated_kernel,
        out_shape=jax.ShapeDtypeStruct(qa.shape, BF16),
        grid=(bsz, A_WIDTH // LANES),
        in_specs=[spec, spec, spec, spec,
                  pl.BlockSpec(bias_first.shape, lambda b, hp: (0, 0, 0)),
                  pl.BlockSpec(bias_later.shape, lambda b, hp: (0, 0, 0))],
        out_specs=spec,
        scratch_shapes=[acc] * 12,
        compiler_params=pltpu.CompilerParams(dimension_semantics=("arbitrary", "arbitrary"),
                                             vmem_limit_bytes=VMEM_LIMIT),
        name="dilated_attn",
    )(qa, ka, va, ga, bias_first, bias_later)


C_TILE = 1024


def _sink_kernel(q_ref, g_ref, kv_ref, kvp_ref, sink_ref, bias_ref, o_ref):
    first_tile = pl.program_id(1) == 0
    lo, mlo, mhi = _head_masks()
    ngrp = C_WIDTH // LANES
    per_call = UNROLL // ngrp
    for u0 in range(0, C_TILE // BLK, per_call):
        blocks = []
        for u in range(u0, u0 + per_call):
            r0 = u * BLK
            cur = kv_ref[r0:r0 + BLK, :]
            if u == 0:
                prev = kvp_ref[...]
                bias = bias_ref[jnp.where(first_tile, 0, 1)]
            else:
                prev = kv_ref[r0 - BLK:r0, :]
                bias = bias_ref[1]
            kb = jnp.concatenate([prev[:, :LANES], cur[:, :LANES]], axis=0)
            vb = jnp.concatenate([prev[:, LANES:], cur[:, LANES:]], axis=0)
            for g in range(ngrp):
                blocks.append((q_ref[r0:r0 + BLK, g * LANES:(g + 1) * LANES], kb, vb, bias))
        outs = _attend_pairs(blocks, lo, mlo, mhi)
        for idx, (mm, ll, acc) in enumerate(outs):
            r0 = (u0 + idx // ngrp) * BLK
            cols = slice((idx % ngrp) * LANES, (idx % ngrp + 1) * LANES)
            sk = sink_ref[idx % ngrp]
            m2 = jnp.maximum(mm, sk)
            e = jnp.exp2(mm - m2)
            den = ll * e + jnp.exp2(sk - m2)
            gate = g_ref[r0:r0 + BLK, cols].astype(F32)
            o_ref[r0:r0 + BLK, cols] = (acc * e / den * gate).astype(BF16)


def _sink_attention(qc, gc, kvc, sink_lanes, bias):
    bsz, seq, _ = qc.shape
    per = C_TILE // BLK
    return pl.pallas_call(
        _sink_kernel,
        out_shape=jax.ShapeDtypeStruct(qc.shape, BF16),
        grid=(bsz, seq // C_TILE),
        in_specs=[
            pl.BlockSpec((None, C_TILE, C_WIDTH), lambda b, i: (b, i, 0)),
            pl.BlockSpec((None, C_TILE, C_WIDTH), lambda b, i: (b, i, 0)),
            pl.BlockSpec((None, C_TILE, 2 * C_KV_WIDTH), lambda b, i: (b, i, 0)),
            pl.BlockSpec((None, BLK, 2 * C_KV_WIDTH), lambda b, i: (b, jnp.maximum(i * per - 1, 0), 0)),
            pl.BlockSpec(sink_lanes.shape, lambda b, i: (0, 0, 0)),
            pl.BlockSpec(bias.shape, lambda b, i: (0, 0, 0)),
        ],
        out_specs=pl.BlockSpec((None, C_TILE, C_WIDTH), lambda b, i: (b, i, 0)),
        compiler_params=pltpu.CompilerParams(dimension_semantics=("arbitrary", "arbitrary"),
                                             vmem_limit_bytes=VMEM_LIMIT),
        name="sink_attn",
    )(qc, gc, kvc, kvc, sink_lanes, bias)


SSD_TILE = 1024
SSD_INTERLEAVE = 4


def _ssd_kernel(xc_ref, gz_ref, dt_ref, dtb_ref, alog_ref, dexp_ref, nw_ref,
                expand_ref, tri_ref, o_ref, state):
    q = SSD_CHUNK
    hg = B_HEADS // B_GROUPS
    gw = B_WIDTH // B_GROUPS

    @pl.when(pl.program_id(1) == 0)
    def _():
        state[...] = jnp.zeros_like(state)

    tri = tri_ref[...]
    causal = tri > 0
    expand = expand_ref[...]
    lo = lax.broadcasted_iota(jnp.int32, (q, LANES), 1) < HEAD_DIM
    a2 = -jnp.exp(alog_ref[...]) * LOG2E

    gcols = [slice(g * gw, (g + 1) * gw) for g in range(B_GROUPS)]
    st = [state[:, gc] for gc in gcols]

    for c_lo in range(0, SSD_TILE // q, SSD_INTERLEAVE):
        chunks = [dict(t0=c * q) for c in range(c_lo, c_lo + SSD_INTERLEAVE)]
        for ch in chunks:
            rows = slice(ch["t0"], ch["t0"] + q)
            ch["xs_b"] = xc_ref[rows, 0:B_WIDTH]
            dt = jax.nn.softplus(dt_ref[rows, :] + dtb_ref[...])
            da = dt * a2
            a_cum = _split_dot_left(tri, da)
            ch.update(dt=dt, a_cum=a_cum, a_cum_t=a_cum.T, dt_t=dt.T)
        for ch in chunks:
            a_cum = ch["a_cum"]
            dte = jnp.exp2(a_cum[q - 1:q, :] - a_cum)
            ch["ea_x"] = _split_dot(jnp.exp2(a_cum), expand)
            w_x = jnp.dot((ch["dt"] * dte).astype(BF16), expand, preferred_element_type=F32)
            ch["xw_b"] = (ch["xs_b"].astype(F32) * w_x).astype(BF16)
        for ch in chunks:
            rows = slice(ch["t0"], ch["t0"] + q)
            ch["cm_b"], ch["cbm"], ch["new"] = [], [], []
            for g in range(B_GROUPS):
                b0 = B_WIDTH + g * B_STATE
                c0 = B_WIDTH + (B_GROUPS + g) * B_STATE
                bm_b = xc_ref[rows, b0:b0 + B_STATE]
                cm_b = xc_ref[rows, c0:c0 + B_STATE]
                ch["cm_b"].append(cm_b)
                ch["cbm"].append(lax.dot_general(cm_b, bm_b, (((1,), (1,)), ((), ())),
                                                 preferred_element_type=F32))
                bm_t = bm_b.astype(F32).T.astype(BF16)
                ch["new"].append(jnp.dot(bm_t, ch["xw_b"][:, gcols[g]], preferred_element_type=F32))
        for ch in chunks:
            ch["y_off"] = []
            for g in range(B_GROUPS):
                ea_g = ch["ea_x"][:, gcols[g]]
                ch["y_off"].append(jnp.dot(ch["cm_b"][g], st[g].astype(BF16), preferred_element_type=F32) * ea_g)
                st[g] = st[g] * ea_g[q - 1:q, :] + ch["new"][g]
        for ch in chunks:
            y_parts = []
            for g in range(B_GROUPS):
                y_diag = []
                for pr in range(hg // 2):
                    h0 = g * hg + 2 * pr
                    mats = []
                    for h in (h0, h0 + 1):
                        seg = ch["a_cum"][:, h:h + 1] - ch["a_cum_t"][h:h + 1, :]
                        decay = jnp.exp2(jnp.where(causal, seg, NEG))
                        mats.append((ch["cbm"][g] * decay * ch["dt_t"][h:h + 1, :]).astype(BF16))
                    lhs = jnp.concatenate(mats, axis=0)
                    x0 = (h0 // 2) * LANES
                    yy = jnp.dot(lhs, ch["xs_b"][:, x0:x0 + LANES], preferred_element_type=F32)
                    y_diag.append(jnp.where(lo, yy[:q], yy[q:]))
                y_parts.append(jnp.concatenate(y_diag, axis=1) + ch["y_off"][g])
            ch["y"] = jnp.concatenate(y_parts, axis=1)
        for ch in chunks:
            rows = slice(ch["t0"], ch["t0"] + q)
            y = ch["y"] + dexp_ref[...] * ch["xs_b"].astype(F32)
            y = y * gz_ref[rows, :].astype(F32)
            outs = []
            for g in range(B_GROUPS):
                yg = y[:, gcols[g]]
                outs.append(yg * lax.rsqrt(jnp.mean(yg * yg, axis=-1, keepdims=True) + NORM_EPS))
            o_ref[rows, :] = (jnp.concatenate(outs, axis=1) * nw_ref[...]).astype(BF16)

    for g in range(B_GROUPS):
        state[:, gcols[g]] = st[g]


def _split_dot_left(w_bf16, x):
    hi = x.astype(BF16)
    lo = (x - hi.astype(F32)).astype(BF16)
    return (jnp.dot(w_bf16, hi, preferred_element_type=F32)
            + jnp.dot(w_bf16, lo, preferred_element_type=F32))


def _ssd(xc, gzb, dt_raw, dt_bias_row, a_log_row, d_exp, norm_w, expand, tri):
    bsz, seq, _ = xc.shape
    const = lambda a: pl.BlockSpec(a.shape, lambda b, i: (0,) * a.ndim)
    return pl.pallas_call(
        _ssd_kernel,
        out_shape=jax.ShapeDtypeStruct((bsz, seq, B_WIDTH), BF16),
        grid=(bsz, seq // SSD_TILE),
        in_specs=[
            pl.BlockSpec((None, SSD_TILE, B_CONV_CH), lambda b, i: (b, i, 0)),
            pl.BlockSpec((None, SSD_TILE, B_WIDTH), lambda b, i: (b, i, 0)),
            pl.BlockSpec((None, SSD_TILE, DT_PAD), lambda b, i: (b, i, 0)),
            const(dt_bias_row), const(a_log_row), const(d_exp), const(norm_w),
            const(expand), const(tri),
        ],
        out_specs=pl.BlockSpec((None, SSD_TILE, B_WIDTH), lambda b, i: (b, i, 0)),
        scratch_shapes=[pltpu.VMEM((B_STATE, B_WIDTH), F32)],
        compiler_params=pltpu.CompilerParams(dimension_semantics=("arbitrary", "arbitrary"),
                                             vmem_limit_bytes=VMEM_LIMIT),
        name="ssd",
    )(xc, gzb, dt_raw, dt_bias_row, a_log_row, d_exp, norm_w, expand, tri)


def _outproj_kernel(ya_ref, yb_ref, yc_ref, x_ref, mod_ref, pw_ref, w_ref, permt_ref, o_ref):
    units = []
    for u in range(PERM_PER_TILE):
        ya_res = jnp.concatenate([ya_ref[r, u * BF16_ROWS:(u + 1) * BF16_ROWS, :] for r in range(RES)], axis=0)
        units.append(jnp.dot(permt_ref[...], ya_res, preferred_element_type=F32).astype(BF16))
    ya = jnp.concatenate(units, axis=0)
    ycat = jnp.concatenate([ya, yb_ref[...], yc_ref[...]], axis=1)
    y = jnp.dot(ycat, w_ref[...], preferred_element_type=F32)
    ms = jnp.mean(y * y, axis=-1, keepdims=True)
    yn = y * lax.rsqrt(ms + NORM_EPS) * pw_ref[...]
    o_ref[...] = x_ref[...] + mod_ref[:, 2 * D_MODEL:] * yn


def _outproj(ya, yb, yc, x, mod_i, post_w, w_out_packed, layer, perm_t):
    bsz, seq, _ = x.shape
    nat_spec = lambda width: pl.BlockSpec((None, ROW_TILE, width), lambda b, i: (b, i, 0))
    return pl.pallas_call(
        _outproj_kernel,
        out_shape=jax.ShapeDtypeStruct(x.shape, F32),
        grid=(bsz, seq // ROW_TILE),
        in_specs=[
            pl.BlockSpec((None, RES, BF16_ROWS * PERM_PER_TILE, A_WIDTH), lambda b, i: (b, 0, i, 0)),
            nat_spec(B_WIDTH), nat_spec(C_WIDTH), nat_spec(D_MODEL),
            pl.BlockSpec((None, 1, 3 * D_MODEL), lambda b, i: (b, 0, 0)),
            pl.BlockSpec((1, D_MODEL), lambda b, i: (0, 0)),
            pl.BlockSpec((None,) + w_out_packed.shape[1:], lambda b, i: (layer, 0, 0)),
            pl.BlockSpec((PERM_TILE, PERM_TILE), lambda b, i: (0, 0)),
        ],
        out_specs=nat_spec(D_MODEL),
        compiler_params=pltpu.CompilerParams(dimension_semantics=("arbitrary", "arbitrary"),
                                             vmem_limit_bytes=VMEM_LIMIT),
        name="out_proj",
    )(ya, yb, yc, x, mod_i, post_w, w_out_packed, perm_t)


def _c_head_order():
    ngrp = C_WIDTH // LANES
    return [g + ngrp * half for g in range(ngrp) for half in range(2)]


def _reorder_c_heads(w, axis):
    heads = [lax.slice_in_dim(w, h * HEAD_DIM, (h + 1) * HEAD_DIM, axis=axis) for h in _c_head_order()]
    return jnp.concatenate(heads, axis=axis)


def _pack_w_in(w):
    dt0 = IN_COLS_MAIN
    c0 = dt0 + B_HEADS
    main = w[:, :, :dt0].astype(BF16)
    t = w[:, :, dt0:].astype(BF16)
    qc = _reorder_c_heads(t[:, :, B_HEADS:B_HEADS + C_WIDTH], 2)
    zc = _reorder_c_heads(t[:, :, B_HEADS + C_WIDTH:B_HEADS + 2 * C_WIDTH], 2)
    kv = t[:, :, B_HEADS + 2 * C_WIDTH:B_HEADS + 2 * C_WIDTH + 2 * C_KV_WIDTH]
    dt = jnp.pad(t[:, :, :B_HEADS], ((0, 0), (0, 0), (0, DT_PAD - B_HEADS)))
    return main, jnp.concatenate([qc, zc, kv, dt], axis=2)


def _pack_w_out(w):
    c0 = A_WIDTH + B_WIDTH
    w = w.astype(BF16)
    return jnp.concatenate([w[:, :c0], _reorder_c_heads(w[:, c0:], 1)], axis=1)


def kernel(x, c, ada_w, ada_b, pre_norm_w, post_norm_w, w_in, conv_w, conv_b, dt_bias, a_log, d_skip,
           ssm_norm_w, sinks, w_out):
    bsz, seq, _ = x.shape
    assert seq % (RES * BLK) == 0 and seq % SSD_TILE == 0 and seq % C_TILE == 0

    mod = _modulation(c, ada_w, ada_b)
    bias_first, bias_later = _dilated_biases()
    kpos = np.arange(2 * BLK) - BLK
    swa = _band_bias(np.arange(BLK), kpos, C_WINDOW)
    swa_first = np.where(kpos[None, :] >= 0, swa, NEG).astype(np.float32)
    bias_c = jnp.asarray(np.stack([swa_first, swa]))
    expand_np = np.zeros((LANES, B_WIDTH), np.float32)
    for h in range(B_HEADS):
        expand_np[h, h * HEAD_DIM:(h + 1) * HEAD_DIM] = 1.0
    expand = jnp.asarray(expand_np).astype(BF16)
    tri = jnp.asarray(np.tril(np.ones((SSD_CHUNK, SSD_CHUNK), np.float32))).astype(BF16)
    perm_np = _residue_perm()
    perm = jnp.asarray(perm_np).astype(BF16)
    perm_t = jnp.asarray(perm_np.T).astype(BF16)
    pad16 = lambda v: jnp.pad(v.astype(F32), (0, DT_PAD - B_HEADS)).reshape(1, DT_PAD)

    w_in_main, w_in_tail = _pack_w_in(w_in)
    w_out_packed = _pack_w_out(w_out)
    for i in range(DEPTH):
        qa, ka, va, ga, gzb, xc, dt_raw, qc, gc, kvc = _inproj(
            x, mod[i], pre_norm_w[i].reshape(1, D_MODEL), w_in_main, w_in_tail, i, perm,
            conv_w[i], conv_b[i].reshape(1, B_CONV_CH))

        ya = _dilated_attention(qa, ka, va, ga, bias_first, bias_later)

        yb = _ssd(xc, gzb, dt_raw, pad16(dt_bias[i]), pad16(a_log[i]),
                  jnp.repeat(d_skip[i].astype(F32), HEAD_DIM).reshape(1, B_WIDTH),
                  ssm_norm_w[i].reshape(1, B_WIDTH), expand, tri)

        sink_lanes = jnp.repeat((sinks[i].astype(F32) * LOG2E).reshape(2, C_HEADS // 2).T, HEAD_DIM, axis=1)
        yc = _sink_attention(qc, gc, kvc, sink_lanes.reshape(C_HEADS // 2, 1, LANES), bias_c)

        x = _outproj(ya, yb, yc, x, mod[i], post_norm_w[i].reshape(1, D_MODEL),
                     w_out_packed, i, perm_t)
    return x
```

```python
import functools
import math

import numpy as np
import jax
import jax.numpy as jnp
from jax import lax
from jax.experimental import pallas as pl
from jax.experimental.pallas import tpu as pltpu

F32 = jnp.float32
BF16 = jnp.bfloat16

D_MODEL = 1024
DEPTH = 4
HEAD_DIM = 64
LANES = 128
BF16_ROWS = 16
A_WIDTH = 512
B_WIDTH = 1024
B_HEADS = 16
B_GROUPS = 2
B_STATE = 128
B_CONV = 4
B_CONV_CH = B_WIDTH + 2 * B_GROUPS * B_STATE
SSD_CHUNK = 128
C_WIDTH = 512
C_HEADS = 8
C_KV_WIDTH = 128
C_WINDOW = 128
A_WINDOW = 128
NORM_EPS = 1e-6
RES = 16
BLK = 128
NEG = -1e30
LOG2E = math.log2(math.e)
DT_PAD = 128
IN_COLS_MAIN = 4 * A_WIDTH + B_WIDTH + B_CONV_CH
IN_COLS_TAIL = 2 * C_WIDTH + 2 * C_KV_WIDTH + DT_PAD
PERM_TILE = RES * BF16_ROWS
PERM_PER_TILE = 2
ROW_TILE = PERM_TILE * PERM_PER_TILE
SEG_COLS = 256
VMEM_LIMIT = 56 * 1024 * 1024
UNROLL = 16


def _silu(v):
    return v * jax.nn.sigmoid(v)


def _split_dot(x, w_bf16):
    hi = x.astype(BF16)
    lo = (x - hi.astype(F32)).astype(BF16)
    return (jnp.dot(hi, w_bf16, preferred_element_type=F32)
            + jnp.dot(lo, w_bf16, preferred_element_type=F32))


def _residue_perm():
    p = np.zeros((PERM_TILE, PERM_TILE), np.float32)
    for r in range(RES):
        for jj in range(BF16_ROWS):
            p[r * BF16_ROWS + jj, RES * jj + r] = 1.0
    return p


def _mod_kernel(c_ref, w_ref, b_ref, o_ref):
    ca = _silu(c_ref[...])
    o_ref[...] = jnp.dot(ca, w_ref[...], preferred_element_type=F32,
                         precision=lax.Precision.HIGHEST) + b_ref[...]


def _modulation(c, ada_w, ada_b):
    bsz = c.shape[0]
    ada_b4 = ada_b.reshape(DEPTH, 3, 1, D_MODEL)
    return pl.pallas_call(
        _mod_kernel,
        out_shape=jax.ShapeDtypeStruct((DEPTH, bsz, 3 * D_MODEL), F32),
        grid=(DEPTH, 3),
        in_specs=[
            pl.BlockSpec((bsz, D_MODEL), lambda i, k: (0, 0)),
            pl.BlockSpec((None, D_MODEL, D_MODEL), lambda i, k: (i, 0, k)),
            pl.BlockSpec((None, None, 1, D_MODEL), lambda i, k: (i, k, 0, 0)),
        ],
        out_specs=pl.BlockSpec((None, bsz, D_MODEL), lambda i, k: (i, 0, k)),
        compiler_params=pltpu.CompilerParams(dimension_semantics=("arbitrary", "arbitrary")),
        name="ada_mod",
    )(c, ada_w, ada_b4).reshape(DEPTH, bsz, 1, 3 * D_MODEL)


def _inproj_kernel(x_ref, mod_ref, pw_ref, w_ref, wt_ref, perm_ref, cw_ref, cb_ref,
                   qa_ref, ka_ref, va_ref, ga_ref, gb_ref, xc_ref, dt_ref,
                   qc_ref, gc_ref, kvc_ref, xpad, h, h_res):
    x = x_ref[...]
    ms = jnp.mean(x * x, axis=-1, keepdims=True)
    xn = x * lax.rsqrt(ms + NORM_EPS) * pw_ref[...]
    shift, scale_mod = mod_ref[:, 0:D_MODEL], mod_ref[:, D_MODEL:2 * D_MODEL]
    h[...] = (xn * (1.0 + scale_mod) + shift).astype(BF16)
    for u in range(PERM_PER_TILE):
        rows = slice(u * PERM_TILE, (u + 1) * PERM_TILE)
        h_res[rows, :] = jnp.dot(perm_ref[...], h[rows, :], preferred_element_type=F32).astype(BF16)

    def proj(lhs, c0, width):
        if c0 >= IN_COLS_MAIN:
            rhs = wt_ref[:, c0 - IN_COLS_MAIN:c0 - IN_COLS_MAIN + width]
        else:
            rhs = w_ref[:, c0:c0 + width]
        return jnp.dot(lhs[...], rhs, preferred_element_type=F32)

    def segments(width):
        return [(s, min(SEG_COLS, width - s)) for s in range(0, width, SEG_COLS)]

    def project_to(ref, lhs, c0, width, post):
        def job(s, w):
            ref[:, s:s + w] = post(proj(lhs, c0 + s, w)).astype(ref.dtype)
        return [functools.partial(job, s, w) for s, w in segments(width)]

    def project_grouped(ref, c0, post):
        def job(s, w):
            val = post(proj(h_res, c0 + s, w)).astype(BF16)
            for u in range(PERM_PER_TILE):
                for r in range(RES):
                    row = u * PERM_TILE + r * BF16_ROWS
                    ref[r, u * BF16_ROWS:(u + 1) * BF16_ROWS, s:s + w] = val[row:row + BF16_ROWS]
        return [functools.partial(job, s, w) for s, w in segments(A_WIDTH)]

    qscale = HEAD_DIM ** -0.5 * LOG2E
    off_zb = 4 * A_WIDTH
    off_xbc = off_zb + B_WIDTH
    off_qc = off_xbc + B_CONV_CH
    off_kv = off_qc + 2 * C_WIDTH
    off_dt = off_kv + 2 * C_KV_WIDTH
    pw = B_CONV_CH // 3

    @pl.when(pl.program_id(1) == 0)
    def _():
        xpad[...] = jnp.zeros_like(xpad)
    sub = lax.broadcasted_iota(jnp.int32, (CONV_PAD, SEG_COLS), 0)

    def conv_job(s, w):
        cols = slice(s, s + w)
        res = proj(h, off_xbc + s, w)
        tail = xpad[:, cols]
        conv = cb_ref[:, cols] + cw_ref[B_CONV - 1:B_CONV, cols] * res
        for k in range(1, B_CONV):
            rolled = pltpu.roll(res, k, 0)
            head = jnp.where(sub < k, pltpu.roll(tail, k, 0), rolled[0:CONV_PAD])
            prev = jnp.concatenate([head, rolled[CONV_PAD:]], axis=0)
            conv = conv + cw_ref[B_CONV - 1 - k:B_CONV - k, cols] * prev
        xc_ref[:, cols] = _silu(conv).astype(BF16)
        xpad[:, cols] = res[ROW_TILE - CONV_PAD:, :]

    scaled = lambda v: v * qscale
    plain = lambda v: v
    heavy = [functools.partial(conv_job, s, w) for s, w in segments(B_CONV_CH)]
    medium = (project_grouped(ga_ref, 3 * A_WIDTH, _silu) + project_to(gb_ref, h, off_zb, B_WIDTH, _silu)
              + project_to(gc_ref, h, off_qc + C_WIDTH, C_WIDTH, _silu))
    light = (project_grouped(ka_ref, A_WIDTH, plain) + project_grouped(va_ref, 2 * A_WIDTH, plain)
             + project_to(kvc_ref, h, off_kv, 2 * C_KV_WIDTH, plain) + project_to(dt_ref, h, off_dt, DT_PAD, plain)
             + project_grouped(qa_ref, 0, scaled) + project_to(qc_ref, h, off_qc, C_WIDTH, scaled))
    order = []
    for i, job in enumerate(heavy):
        order += [job, light[i]]
    rest = light[len(heavy):]
    for i, job in enumerate(medium):
        order.append(job)
        if i < len(rest):
            order.append(rest[i])
    for job in order:
        job()


CONV_PAD = 8


def _inproj(x, mod_i, pre_w, w_main, w_tail, layer, perm, conv_w, conv_b):
    bsz, seq, _ = x.shape
    jn = seq // RES
    nat_spec = lambda width: pl.BlockSpec((None, ROW_TILE, width), lambda b, i: (b, i, 0))
    grp_spec = pl.BlockSpec((None, RES, BF16_ROWS * PERM_PER_TILE, A_WIDTH), lambda b, i: (b, 0, i, 0))
    nat = lambda width, dt: jax.ShapeDtypeStruct((bsz, seq, width), dt)
    grp = jax.ShapeDtypeStruct((bsz, RES, jn, A_WIDTH), BF16)
    return pl.pallas_call(
        _inproj_kernel,
        out_shape=[grp, grp, grp, grp,
                   nat(B_WIDTH, BF16), nat(B_CONV_CH, BF16), nat(DT_PAD, F32),
                   nat(C_WIDTH, BF16), nat(C_WIDTH, BF16), nat(2 * C_KV_WIDTH, BF16)],
        grid=(bsz, seq // ROW_TILE),
        in_specs=[
            nat_spec(D_MODEL),
            pl.BlockSpec((None, 1, 3 * D_MODEL), lambda b, i: (b, 0, 0)),
            pl.BlockSpec((1, D_MODEL), lambda b, i: (0, 0)),
            pl.BlockSpec((None, D_MODEL, IN_COLS_MAIN), lambda b, i: (layer, 0, 0),
                         pipeline_mode=pl.Buffered(1)),
            pl.BlockSpec((None, D_MODEL, IN_COLS_TAIL), lambda b, i: (layer, 0, 0),
                         pipeline_mode=pl.Buffered(1)),
            pl.BlockSpec((PERM_TILE, PERM_TILE), lambda b, i: (0, 0)),
            pl.BlockSpec(conv_w.shape, lambda b, i: (0, 0)),
            pl.BlockSpec(conv_b.shape, lambda b, i: (0, 0)),
        ],
        out_specs=[grp_spec, grp_spec, grp_spec, grp_spec,
                   nat_spec(B_WIDTH), nat_spec(B_CONV_CH), nat_spec(DT_PAD),
                   nat_spec(C_WIDTH), nat_spec(C_WIDTH), nat_spec(2 * C_KV_WIDTH)],
        scratch_shapes=[pltpu.VMEM((CONV_PAD, B_CONV_CH), F32),
                        pltpu.VMEM((ROW_TILE, D_MODEL), BF16), pltpu.VMEM((ROW_TILE, D_MODEL), BF16)],
        compiler_params=pltpu.CompilerParams(dimension_semantics=("arbitrary", "arbitrary"),
                                             vmem_limit_bytes=VMEM_LIMIT),
        name="in_proj",
    )(x, mod_i, pre_w, w_main, w_tail, perm, conv_w, conv_b)


def _head_masks():
    lane = lax.broadcasted_iota(jnp.int32, (BLK, LANES), 1)
    lo = lane < HEAD_DIM
    return lo, lo.astype(F32).astype(BF16), (~lo).astype(F32).astype(BF16)


def _attend_pairs(blocks, lo, mlo, mhi):
    scores = []
    for qb, kb, _, bias2 in blocks:
        qs = jnp.concatenate([qb * mlo, qb * mhi], axis=0)
        scores.append(lax.dot_general(qs, kb, (((1,), (1,)), ((), ())),
                                      preferred_element_type=F32) + bias2)
    outs = []
    for (_, _, vb, _), s in zip(blocks, scores):
        m = jnp.max(s, axis=-1, keepdims=True)
        p = jnp.exp2(s - m).astype(BF16)
        v_ext = jnp.concatenate([vb, jnp.ones_like(vb)], axis=1)
        pv = jnp.dot(p, v_ext, preferred_element_type=F32)
        outs.append((jnp.where(lo, m[:BLK], m[BLK:]),
                     jnp.where(lo, pv[:BLK, LANES:], pv[BLK:, LANES:]),
                     jnp.where(lo, pv[:BLK, :LANES], pv[BLK:, :LANES])))
    return outs


def _band_bias(qpos, kpos, window):
    dist = qpos[:, None] - kpos[None, :]
    ok = (dist >= 0) & (dist <= window)
    b = np.where(ok, 0.0, NEG).astype(np.float32)
    return np.concatenate([b, b], axis=0)


def _dilated_biases():
    first, later = [], []
    r, jj = np.meshgrid(np.arange(RES), np.arange(8), indexing="ij")
    qpos = (16 * jj + r).reshape(-1)
    r, jj = np.meshgrid(np.arange(RES), np.arange(16), indexing="ij")
    kpos = (16 * (jj - 8) + r).reshape(-1)
    first.append(_band_bias(qpos, qpos, A_WINDOW))
    later.append(_band_bias(qpos, kpos, A_WINDOW))
    rh, jj = np.meshgrid(np.arange(4), np.arange(32), indexing="ij")
    qpos = (4 * jj + rh).reshape(-1)
    rh, jj = np.meshgrid(np.arange(4), np.arange(64), indexing="ij")
    kpos = (4 * (jj - 32) + rh).reshape(-1)
    first.append(_band_bias(qpos, qpos, A_WINDOW))
    later.append(_band_bias(qpos, kpos, A_WINDOW))
    qpos = np.arange(BLK)
    kpos = np.arange(2 * BLK) - BLK
    first.append(_band_bias(qpos, qpos, A_WINDOW))
    later.append(_band_bias(qpos, kpos, A_WINDOW))
    return jnp.asarray(np.stack(first)), jnp.asarray(np.stack(later))


def _dilated_kernel(q_ref, k_ref, v_ref, g_ref, b1_ref, b2_ref, o_ref,
                    qf, kf, vf, m1, l1, a1, m2, l2, a2, m3, l3, a3):
    jn = q_ref.shape[1]
    lo, mlo, mhi = _head_masks()
    attend = functools.partial(_attend_pairs, lo=lo, mlo=mlo, mhi=mhi)

    def p3_blocks(r):
        blocks = [(q_ref[r, 0:BLK, :], k_ref[r, 0:BLK, :], v_ref[r, 0:BLK, :], b1_ref[2])]
        for jb in range(1, jn // BLK):
            s0 = (jb - 1) * BLK
            blocks.append((q_ref[r, s0 + BLK:s0 + 2 * BLK, :], k_ref[r, s0:s0 + 2 * BLK, :],
                           v_ref[r, s0:s0 + 2 * BLK, :], b2_ref[2]))
        return blocks

    def p3_store(r, outs):
        for jb, (mm, ll, acc) in enumerate(outs):
            m3[r, jb * BLK:(jb + 1) * BLK, :] = mm
            l3[r, jb * BLK:(jb + 1) * BLK, :] = ll
            a3[r, jb * BLK:(jb + 1) * BLK, :] = acc

    per3 = jn // BLK
    res3 = UNROLL // per3

    def p3_body(rr, carry):
        rs = [rr * res3 + d for d in range(res3)]
        outs = attend([blk for r in rs for blk in p3_blocks(r)])
        for d, r in enumerate(rs):
            p3_store(r, outs[d * per3:(d + 1) * per3])
        return carry
    lax.fori_loop(0, RES // res3, p3_body, 0)

    def p2_rows(ref, r4, start, size):
        return jnp.concatenate([ref[4 * rh + r4, pl.ds(start, size), :] for rh in range(4)], axis=0)

    def p2_store(r4, start, out):
        mm, ll, acc = out
        for rh in range(4):
            sl = slice(32 * rh, 32 * rh + 32)
            m2[4 * rh + r4, pl.ds(start, 32), :] = mm[sl]
            l2[4 * rh + r4, pl.ds(start, 32), :] = ll[sl]
            a2[4 * rh + r4, pl.ds(start, 32), :] = acc[sl]

    def p2_first(r4):
        return (p2_rows(q_ref, r4, 0, 32), p2_rows(k_ref, r4, 0, 32), p2_rows(v_ref, r4, 0, 32), b1_ref[1])

    def p2_later(r4, q0):
        k0 = q0 - 32 if isinstance(q0, int) else pl.multiple_of(q0 - 32, 32)
        return (p2_rows(q_ref, r4, q0, 32), p2_rows(k_ref, r4, k0, 64), p2_rows(v_ref, r4, k0, 64), b2_ref[1])

    steps2 = UNROLL // 4
    outs = attend([p2_first(r4) for r4 in range(4)]
                  + [p2_later(r4, 32 * b4) for b4 in range(1, steps2) for r4 in range(4)])
    for b4 in range(steps2):
        for r4 in range(4):
            p2_store(r4, 32 * b4, outs[4 * b4 + r4])

    def p2_body(i, carry):
        starts = [pl.multiple_of((i * steps2 + d) * 32, 32) for d in range(steps2)]
        outs = attend([p2_later(r4, q0) for q0 in starts for r4 in range(4)])
        for d, q0 in enumerate(starts):
            for r4 in range(4):
                p2_store(r4, q0, outs[4 * d + r4])
        return carry
    lax.fori_loop(1, jn // (32 * steps2), p2_body, 0)

    for r in range(RES):
        qf[r] = q_ref[r].astype(F32)
        kf[r] = k_ref[r].astype(F32)
        vf[r] = v_ref[r].astype(F32)

    def p1_rows(ref, start, size):
        return jnp.concatenate([ref[r, pl.ds(start, size), :] for r in range(RES)], axis=0).astype(BF16)

    def p1_store(start, out):
        mm, ll, acc = out
        for r in range(RES):
            sl = slice(8 * r, 8 * r + 8)
            m1[r, pl.ds(start, 8), :] = mm[sl]
            l1[r, pl.ds(start, 8), :] = ll[sl]
            a1[r, pl.ds(start, 8), :] = acc[sl]

    def p1_later(q0):
        k0 = q0 - 8 if isinstance(q0, int) else pl.multiple_of(q0 - 8, 8)
        return (p1_rows(qf, q0, 8), p1_rows(kf, k0, 16), p1_rows(vf, k0, 16), b2_ref[0])

    outs = attend([(p1_rows(qf, 0, 8), p1_rows(kf, 0, 8), p1_rows(vf, 0, 8), b1_ref[0])]
                  + [p1_later(8 * a) for a in range(1, UNROLL)])
    for a in range(UNROLL):
        p1_store(8 * a, outs[a])

    def p1_body(i, carry):
        starts = [pl.multiple_of((i * UNROLL + d) * 8, 8) for d in range(UNROLL)]
        outs = attend([p1_later(q0) for q0 in starts])
        for q0, out in zip(starts, outs):
            p1_store(q0, out)
        return carry
    lax.fori_loop(1, jn // (8 * UNROLL), p1_body, 0)

    def merge_body(r, carry):
        ma, mb, mc = m1[r], m2[r], m3[r]
        mx = jnp.maximum(jnp.maximum(ma, mb), mc)
        ea, eb, ec = jnp.exp2(ma - mx), jnp.exp2(mb - mx), jnp.exp2(mc - mx)
        num = ea * a1[r] + eb * a2[r] + ec * a3[r]
        den = ea * l1[r] + eb * l2[r] + ec * l3[r]
        o_ref[r] = (num / den * g_ref[r].astype(F32)).astype(BF16)
        return carry
    lax.fori_loop(0, RES, merge_body, 0)


def _dilated_attention(qa, ka, va, ga, bias_first, bias_later):
    bsz, _, jn, _ = qa.shape
    assert jn % (8 * UNROLL) == 0 and jn % BLK == 0
    spec = pl.BlockSpec((None, RES, jn, LANES), lambda b, hp: (b, 0, 0, hp))
    acc = pltpu.VMEM((RES, jn, LANES), F32)
    return pl.pallas_call(
        _dilated_kernel,
        out_shape=jax.ShapeDtypeStruct(qa.shape, BF16),
        grid=(bsz, A_WIDTH // LANES),
        in_specs=[spec, spec, spec, spec,
                  pl.BlockSpec(bias_first.shape, lambda b, hp: (0, 0, 0)),
                  pl.BlockSpec(bias_later.shape, lambda b, hp: (0, 0, 0))],
        out_specs=spec,
        scratch_shapes=[acc] * 12,
        compiler_params=pltpu.CompilerParams(dimension_semantics=("arbitrary", "arbitrary"),
                                             vmem_limit_bytes=VMEM_LIMIT),
        name="dilated_attn",
    )(qa, ka, va, ga, bias_first, bias_later)


C_TILE = 1024


def _sink_kernel(q_ref, g_ref, kv_ref, kvp_ref, sink_ref, bias_ref, o_ref):
    first_tile = pl.program_id(1) == 0
    lo, mlo, mhi = _head_masks()
    ngrp = C_WIDTH // LANES
    per_call = UNROLL // ngrp
    for u0 in range(0, C_TILE // BLK, per_call):
        blocks = []
        for u in range(u0, u0 + per_call):
            r0 = u * BLK
            cur = kv_ref[r0:r0 + BLK, :]
            if u == 0:
                prev = kvp_ref[...]
                bias = bias_ref[jnp.where(first_tile, 0, 1)]
            else:
                prev = kv_ref[r0 - BLK:r0, :]
                bias = bias_ref[1]
            kb = jnp.concatenate([prev[:, :LANES], cur[:, :LANES]], axis=0)
            vb = jnp.concatenate([prev[:, LANES:], cur[:, LANES:]], axis=0)
            for g in range(ngrp):
                blocks.append((q_ref[r0:r0 + BLK, g * LANES:(g + 1) * LANES], kb, vb, bias))
        outs = _attend_pairs(blocks, lo, mlo, mhi)
        for idx, (mm, ll, acc) in enumerate(outs):
            r0 = (u0 + idx // ngrp) * BLK
            cols = slice((idx % ngrp) * LANES, (idx % ngrp + 1) * LANES)
            sk = sink_ref[idx % ngrp]
            m2 = jnp.maximum(mm, sk)
            e = jnp.exp2(mm - m2)
            den = ll * e + jnp.exp2(sk - m2)
            gate = g_ref[r0:r0 + BLK, cols].astype(F32)
            o_ref[r0:r0 + BLK, cols] = (acc * e / den * gate).astype(BF16)


def _sink_attention(qc, gc, kvc, sink_lanes, bias):
    bsz, seq, _ = qc.shape
    per = C_TILE // BLK
    return pl.pallas_call(
        _sink_kernel,
        out_shape=jax.ShapeDtypeStruct(qc.shape, BF16),
        grid=(bsz, seq // C_TILE),
        in_specs=[
            pl.BlockSpec((None, C_TILE, C_WIDTH), lambda b, i: (b, i, 0)),
            pl.BlockSpec((None, C_TILE, C_WIDTH), lambda b, i: (b, i, 0)),
            pl.BlockSpec((None, C_TILE, 2 * C_KV_WIDTH), lambda b, i: (b, i, 0)),
            pl.BlockSpec((None, BLK, 2 * C_KV_WIDTH), lambda b, i: (b, jnp.maximum(i * per - 1, 0), 0)),
            pl.BlockSpec(sink_lanes.shape, lambda b, i: (0, 0, 0)),
            pl.BlockSpec(bias.shape, lambda b, i: (0, 0, 0)),
        ],
        out_specs=pl.BlockSpec((None, C_TILE, C_WIDTH), lambda b, i: (b, i, 0)),
        compiler_params=pltpu.CompilerParams(dimension_semantics=("arbitrary", "arbitrary"),
                                             vmem_limit_bytes=VMEM_LIMIT),
        name="sink_attn",
    )(qc, gc, kvc, kvc, sink_lanes, bias)


SSD_TILE = 1024
SSD_INTERLEAVE = 4


def _ssd_kernel(xc_ref, gz_ref, dt_ref, dtb_ref, alog_ref, dexp_ref, nw_ref,
                expand_ref, tri_ref, o_ref, state):
    q = SSD_CHUNK
    hg = B_HEADS // B_GROUPS
    gw = B_WIDTH // B_GROUPS

    @pl.when(pl.program_id(1) == 0)
    def _():
        state[...] = jnp.zeros_like(state)

    tri = tri_ref[...]
    causal = tri > 0
    expand = expand_ref[...]
    lo = lax.broadcasted_iota(jnp.int32, (q, LANES), 1) < HEAD_DIM
    a2 = -jnp.exp(alog_ref[...]) * LOG2E

    gcols = [slice(g * gw, (g + 1) * gw) for g in range(B_GROUPS)]
    st = [state[:, gc] for gc in gcols]

    for c_lo in range(0, SSD_TILE // q, SSD_INTERLEAVE):
        chunks = [dict(t0=c * q) for c in range(c_lo, c_lo + SSD_INTERLEAVE)]
        for ch in chunks:
            rows = slice(ch["t0"], ch["t0"] + q)
            ch["xs_b"] = xc_ref[rows, 0:B_WIDTH]
            dt = jax.nn.softplus(dt_ref[rows, :] + dtb_ref[...])
            da = dt * a2
            a_cum = _split_dot_left(tri, da)
            ch.update(dt=dt, a_cum=a_cum, a_cum_t=a_cum.T, dt_t=dt.T)
        for ch in chunks:
            a_cum = ch["a_cum"]
            dte = jnp.exp2(a_cum[q - 1:q, :] - a_cum)
            ch["ea_x"] = _split_dot(jnp.exp2(a_cum), expand)
            w_x = jnp.dot((ch["dt"] * dte).astype(BF16), expand, preferred_element_type=F32)
            ch["xw_b"] = (ch["xs_b"].astype(F32) * w_x).astype(BF16)
        for ch in chunks:
            rows = slice(ch["t0"], ch["t0"] + q)
            ch["cm_b"], ch["cbm"], ch["new"] = [], [], []
            for g in range(B_GROUPS):
                b0 = B_WIDTH + g * B_STATE
                c0 = B_WIDTH + (B_GROUPS + g) * B_STATE
                bm_b = xc_ref[rows, b0:b0 + B_STATE]
                cm_b = xc_ref[rows, c0:c0 + B_STATE]
                ch["cm_b"].append(cm_b)
                ch["cbm"].append(lax.dot_general(cm_b, bm_b, (((1,), (1,)), ((), ())),
                                                 preferred_element_type=F32))
                bm_t = bm_b.astype(F32).T.astype(BF16)
                ch["new"].append(jnp.dot(bm_t, ch["xw_b"][:, gcols[g]], preferred_element_type=F32))
        for ch in chunks:
            ch["y_off"] = []
            for g in range(B_GROUPS):
                ea_g = ch["ea_x"][:, gcols[g]]
                ch["y_off"].append(jnp.dot(ch["cm_b"][g], st[g].astype(BF16), preferred_element_type=F32) * ea_g)
                st[g] = st[g] * ea_g[q - 1:q, :] + ch["new"][g]
        for ch in chunks:
            y_parts = []
            for g in range(B_GROUPS):
                y_diag = []
                for pr in range(hg // 2):
                    h0 = g * hg + 2 * pr
                    mats = []
                    for h in (h0, h0 + 1):
                        seg = ch["a_cum"][:, h:h + 1] - ch["a_cum_t"][h:h + 1, :]
                        decay = jnp.exp2(jnp.where(causal, seg, NEG))
                        mats.append((ch["cbm"][g] * decay * ch["dt_t"][h:h + 1, :]).astype(BF16))
                    lhs = jnp.concatenate(mats, axis=0)
                    x0 = (h0 // 2) * LANES
                    yy = jnp.dot(lhs, ch["xs_b"][:, x0:x0 + LANES], preferred_element_type=F32)
                    y_diag.append(jnp.where(lo, yy[:q], yy[q:]))
                y_parts.append(jnp.concatenate(y_diag, axis=1) + ch["y_off"][g])
            ch["y"] = jnp.concatenate(y_parts, axis=1)
        for ch in chunks:
            rows = slice(ch["t0"], ch["t0"] + q)
            y = ch["y"] + dexp_ref[...] * ch["xs_b"].astype(F32)
            y = y * gz_ref[rows, :].astype(F32)
            outs = []
            for g in range(B_GROUPS):
                yg = y[:, gcols[g]]
                outs.append(yg * lax.rsqrt(jnp.mean(yg * yg, axis=-1, keepdims=True) + NORM_EPS))
            o_ref[rows, :] = (jnp.concatenate(outs, axis=1) * nw_ref[...]).astype(BF16)

    for g in range(B_GROUPS):
        state[:, gcols[g]] = st[g]


def _split_dot_left(w_bf16, x):
    hi = x.astype(BF16)
    lo = (x - hi.astype(F32)).astype(BF16)
    return (jnp.dot(w_bf16, hi, preferred_element_type=F32)
            + jnp.dot(w_bf16, lo, preferred_element_type=F32))


def _ssd(xc, gzb, dt_raw, dt_bias_row, a_log_row, d_exp, norm_w, expand, tri):
    bsz, seq, _ = xc.shape
    const = lambda a: pl.BlockSpec(a.shape, lambda b, i: (0,) * a.ndim)
    return pl.pallas_call(
        _ssd_kernel,
        out_shape=jax.ShapeDtypeStruct((bsz, seq, B_WIDTH), BF16),
        grid=(bsz, seq // SSD_TILE),
        in_specs=[
            pl.BlockSpec((None, SSD_TILE, B_CONV_CH), lambda b, i: (b, i, 0)),
            pl.BlockSpec((None, SSD_TILE, B_WIDTH), lambda b, i: (b, i, 0)),
            pl.BlockSpec((None, SSD_TILE, DT_PAD), lambda b, i: (b, i, 0)),
            const(dt_bias_row), const(a_log_row), const(d_exp), const(norm_w),
            const(expand), const(tri),
        ],
        out_specs=pl.BlockSpec((None, SSD_TILE, B_WIDTH), lambda b, i: (b, i, 0)),
        scratch_shapes=[pltpu.VMEM((B_STATE, B_WIDTH), F32)],
        compiler_params=pltpu.CompilerParams(dimension_semantics=("arbitrary", "arbitrary"),
                                             vmem_limit_bytes=VMEM_LIMIT),
        name="ssd",
    )(xc, gzb, dt_raw, dt_bias_row, a_log_row, d_exp, norm_w, expand, tri)


OUT_UNITS = 2
OUT_TILE = ROW_TILE * OUT_UNITS


def _outproj_kernel(ya_ref, yb_ref, yc_ref, x_ref, mod_ref, pw_ref, w_ref, permt_ref, o_ref):
    for unit in range(OUT_UNITS):
        rows = slice(unit * ROW_TILE, (unit + 1) * ROW_TILE)
        parts = []
        for u in range(unit * PERM_PER_TILE, (unit + 1) * PERM_PER_TILE):
            ya_res = jnp.concatenate([ya_ref[r, u * BF16_ROWS:(u + 1) * BF16_ROWS, :] for r in range(RES)], axis=0)
            parts.append(jnp.dot(permt_ref[...], ya_res, preferred_element_type=F32).astype(BF16))
        ya = jnp.concatenate(parts, axis=0)
        ycat = jnp.concatenate([ya, yb_ref[rows, :], yc_ref[rows, :]], axis=1)
        y = jnp.dot(ycat, w_ref[...], preferred_element_type=F32)
        ms = jnp.mean(y * y, axis=-1, keepdims=True)
        yn = y * lax.rsqrt(ms + NORM_EPS) * pw_ref[...]
        o_ref[rows, :] = x_ref[rows, :] + mod_ref[:, 2 * D_MODEL:] * yn


def _outproj(ya, yb, yc, x, mod_i, post_w, w_out_packed, layer, perm_t):
    bsz, seq, _ = x.shape
    nat_spec = lambda width: pl.BlockSpec((None, OUT_TILE, width), lambda b, i: (b, i, 0))
    return pl.pallas_call(
        _outproj_kernel,
        out_shape=jax.ShapeDtypeStruct(x.shape, F32),
        grid=(bsz, seq // OUT_TILE),
        in_specs=[
            pl.BlockSpec((None, RES, BF16_ROWS * PERM_PER_TILE * OUT_UNITS, A_WIDTH), lambda b, i: (b, 0, i, 0)),
            nat_spec(B_WIDTH), nat_spec(C_WIDTH), nat_spec(D_MODEL),
            pl.BlockSpec((None, 1, 3 * D_MODEL), lambda b, i: (b, 0, 0)),
            pl.BlockSpec((1, D_MODEL), lambda b, i: (0, 0)),
            pl.BlockSpec((None,) + w_out_packed.shape[1:], lambda b, i: (layer, 0, 0)),
            pl.BlockSpec((PERM_TILE, PERM_TILE), lambda b, i: (0, 0)),
        ],
        out_specs=nat_spec(D_MODEL),
        compiler_params=pltpu.CompilerParams(dimension_semantics=("arbitrary", "arbitrary"),
                                             vmem_limit_bytes=VMEM_LIMIT),
        name="out_proj",
    )(ya, yb, yc, x, mod_i, post_w, w_out_packed, perm_t)


def _c_head_order():
    ngrp = C_WIDTH // LANES
    return [g + ngrp * half for g in range(ngrp) for half in range(2)]


def _reorder_c_heads(w, axis):
    heads = [lax.slice_in_dim(w, h * HEAD_DIM, (h + 1) * HEAD_DIM, axis=axis) for h in _c_head_order()]
    return jnp.concatenate(heads, axis=axis)


def _pack_w_in(w):
    dt0 = IN_COLS_MAIN
    c0 = dt0 + B_HEADS
    main = w[:, :, :dt0].astype(BF16)
    t = w[:, :, dt0:].astype(BF16)
    qc = _reorder_c_heads(t[:, :, B_HEADS:B_HEADS + C_WIDTH], 2)
    zc = _reorder_c_heads(t[:, :, B_HEADS + C_WIDTH:B_HEADS + 2 * C_WIDTH], 2)
    kv = t[:, :, B_HEADS + 2 * C_WIDTH:B_HEADS + 2 * C_WIDTH + 2 * C_KV_WIDTH]
    dt = jnp.pad(t[:, :, :B_HEADS], ((0, 0), (0, 0), (0, DT_PAD - B_HEADS)))
    return main, jnp.concatenate([qc, zc, kv, dt], axis=2)


def _pack_w_out(w):
    c0 = A_WIDTH + B_WIDTH
    w = w.astype(BF16)
    return jnp.concatenate([w[:, :c0], _reorder_c_heads(w[:, c0:], 1)], axis=1)


def kernel(x, c, ada_w, ada_b, pre_norm_w, post_norm_w, w_in, conv_w, conv_b, dt_bias, a_log, d_skip,
           ssm_norm_w, sinks, w_out):
    bsz, seq, _ = x.shape
    assert seq % (RES * BLK) == 0 and seq % SSD_TILE == 0 and seq % C_TILE == 0

    mod = _modulation(c, ada_w, ada_b)
    bias_first, bias_later = _dilated_biases()
    kpos = np.arange(2 * BLK) - BLK
    swa = _band_bias(np.arange(BLK), kpos, C_WINDOW)
    swa_first = np.where(kpos[None, :] >= 0, swa, NEG).astype(np.float32)
    bias_c = jnp.asarray(np.stack([swa_first, swa]))
    expand_np = np.zeros((LANES, B_WIDTH), np.float32)
    for h in range(B_HEADS):
        expand_np[h, h * HEAD_DIM:(h + 1) * HEAD_DIM] = 1.0
    expand = jnp.asarray(expand_np).astype(BF16)
    tri = jnp.asarray(np.tril(np.ones((SSD_CHUNK, SSD_CHUNK), np.float32))).astype(BF16)
    perm_np = _residue_perm()
    perm = jnp.asarray(perm_np).astype(BF16)
    perm_t = jnp.asarray(perm_np.T).astype(BF16)
    pad16 = lambda v: jnp.pad(v.astype(F32), (0, DT_PAD - B_HEADS)).reshape(1, DT_PAD)

    w_in_main, w_in_tail = _pack_w_in(w_in)
    w_out_packed = _pack_w_out(w_out)
    for i in range(DEPTH):
        qa, ka, va, ga, gzb, xc, dt_raw, qc, gc, kvc = _inproj(
            x, mod[i], pre_norm_w[i].reshape(1, D_MODEL), w_in_main, w_in_tail, i, perm,
            conv_w[i], conv_b[i].reshape(1, B_CONV_CH))

        ya = _dilated_attention(qa, ka, va, ga, bias_first, bias_later)

        yb = _ssd(xc, gzb, dt_raw, pad16(dt_bias[i]), pad16(a_log[i]),
                  jnp.repeat(d_skip[i].astype(F32), HEAD_DIM).reshape(1, B_WIDTH),
                  ssm_norm_w[i].reshape(1, B_WIDTH), expand, tri)

        sink_lanes = jnp.repeat((sinks[i].astype(F32) * LOG2E).reshape(2, C_HEADS // 2).T, HEAD_DIM, axis=1)
        yc = _sink_attention(qc, gc, kvc, sink_lanes.reshape(C_HEADS // 2, 1, LANES), bias_c)

        x = _outproj(ya, yb, yc, x, mod[i], post_norm_w[i].reshape(1, D_MODEL),
                     w_out_packed, i, perm_t)
    return x
```

```python
import functools
import math

import numpy as np
import jax
import jax.numpy as jnp
from jax import lax
from jax.experimental import pallas as pl
from jax.experimental.pallas import tpu as pltpu

F32 = jnp.float32
BF16 = jnp.bfloat16

D_MODEL = 1024
DEPTH = 4
HEAD_DIM = 64
LANES = 128
BF16_ROWS = 16
A_WIDTH = 512
B_WIDTH = 1024
B_HEADS = 16
B_GROUPS = 2
B_STATE = 128
B_CONV = 4
B_CONV_CH = B_WIDTH + 2 * B_GROUPS * B_STATE
SSD_CHUNK = 128
C_WIDTH = 512
C_HEADS = 8
C_KV_WIDTH = 128
C_WINDOW = 128
A_WINDOW = 128
NORM_EPS = 1e-6
RES = 16
BLK = 128
NEG = -1e30
LOG2E = math.log2(math.e)
DT_PAD = 128
IN_COLS_MAIN = 4 * A_WIDTH + B_WIDTH + B_CONV_CH
IN_COLS_TAIL = 2 * C_WIDTH + 2 * C_KV_WIDTH + DT_PAD
PERM_TILE = RES * BF16_ROWS
PERM_PER_TILE = 2
ROW_TILE = PERM_TILE * PERM_PER_TILE
SEG_COLS = 256
VMEM_LIMIT = 56 * 1024 * 1024
UNROLL = 16


def _silu(v):
    return v * jax.nn.sigmoid(v)


def _split_dot(x, w_bf16):
    hi = x.astype(BF16)
    lo = (x - hi.astype(F32)).astype(BF16)
    return (jnp.dot(hi, w_bf16, preferred_element_type=F32)
            + jnp.dot(lo, w_bf16, preferred_element_type=F32))


def _residue_perm():
    p = np.zeros((PERM_TILE, PERM_TILE), np.float32)
    for r in range(RES):
        for jj in range(BF16_ROWS):
            p[r * BF16_ROWS + jj, RES * jj + r] = 1.0
    return p


def _mod_kernel(c_ref, w_ref, b_ref, o_ref):
    ca = _silu(c_ref[...])
    o_ref[...] = jnp.dot(ca, w_ref[...], preferred_element_type=F32,
                         precision=lax.Precision.HIGHEST) + b_ref[...]


def _modulation(c, ada_w, ada_b):
    bsz = c.shape[0]
    ada_b4 = ada_b.reshape(DEPTH, 3, 1, D_MODEL)
    return pl.pallas_call(
        _mod_kernel,
        out_shape=jax.ShapeDtypeStruct((DEPTH, bsz, 3 * D_MODEL), F32),
        grid=(DEPTH, 3),
        in_specs=[
            pl.BlockSpec((bsz, D_MODEL), lambda i, k: (0, 0)),
            pl.BlockSpec((None, D_MODEL, D_MODEL), lambda i, k: (i, 0, k)),
            pl.BlockSpec((None, None, 1, D_MODEL), lambda i, k: (i, k, 0, 0)),
        ],
        out_specs=pl.BlockSpec((None, bsz, D_MODEL), lambda i, k: (i, 0, k)),
        compiler_params=pltpu.CompilerParams(dimension_semantics=("arbitrary", "arbitrary")),
        name="ada_mod",
    )(c, ada_w, ada_b4).reshape(DEPTH, bsz, 1, 3 * D_MODEL)


def _inproj_kernel(x_ref, mod_ref, pw_ref, w32_ref, wt_ref, perm_ref, cw_ref, cb_ref,
                   qa_ref, ka_ref, va_ref, ga_ref, gb_ref, xc_ref, dt_ref,
                   qc_ref, gc_ref, kvc_ref, xpad, h, h_res, w_ref):
    @pl.when((pl.program_id(0) == 0) & (pl.program_id(1) == 0))
    def _():
        for c0 in range(0, IN_COLS_MAIN, 2 * SEG_COLS):
            w_ref[:, c0:c0 + 2 * SEG_COLS] = w32_ref[:, c0:c0 + 2 * SEG_COLS].astype(BF16)

    @pl.when(pl.program_id(1) == 0)
    def _():
        xpad[...] = jnp.zeros_like(xpad)

    qscale = HEAD_DIM ** -0.5 * LOG2E
    off_zb = 4 * A_WIDTH
    off_xbc = off_zb + B_WIDTH
    off_qc = off_xbc + B_CONV_CH
    off_kv = off_qc + 2 * C_WIDTH
    off_dt = off_kv + 2 * C_KV_WIDTH
    shift, scale_mod = mod_ref[:, 0:D_MODEL], mod_ref[:, D_MODEL:2 * D_MODEL]
    sub = lax.broadcasted_iota(jnp.int32, (CONV_PAD, SEG_COLS), 0)
    scaled = lambda v: v * qscale
    plain = lambda v: v

    def segments(width):
        return [(s, min(SEG_COLS, width - s)) for s in range(0, width, SEG_COLS)]

    for u in range(PERM_PER_TILE):
        rows = slice(u * PERM_TILE, (u + 1) * PERM_TILE)
        x = x_ref[rows, :]
        ms = jnp.mean(x * x, axis=-1, keepdims=True)
        xn = x * lax.rsqrt(ms + NORM_EPS) * pw_ref[...]
        h[rows, :] = (xn * (1.0 + scale_mod) + shift).astype(BF16)
        h_res[rows, :] = jnp.dot(perm_ref[...], h[rows, :], preferred_element_type=F32).astype(BF16)

        def proj(lhs, c0, width, rows=rows):
            if c0 >= IN_COLS_MAIN:
                rhs = wt_ref[:, c0 - IN_COLS_MAIN:c0 - IN_COLS_MAIN + width]
            else:
                rhs = w_ref[:, c0:c0 + width]
            return jnp.dot(lhs[rows, :], rhs, preferred_element_type=F32)

        def project_to(ref, c0, width, post, rows=rows, proj=proj):
            def job(s, w):
                ref[rows, s:s + w] = post(proj(h, c0 + s, w)).astype(ref.dtype)
            return [functools.partial(job, s, w) for s, w in segments(width)]

        def project_grouped(ref, c0, post, u=u, proj=proj):
            def job(s, w):
                val = post(proj(h_res, c0 + s, w)).astype(BF16)
                for r in range(RES):
                    ref[r, u * BF16_ROWS:(u + 1) * BF16_ROWS, s:s + w] = val[r * BF16_ROWS:(r + 1) * BF16_ROWS]
            return [functools.partial(job, s, w) for s, w in segments(A_WIDTH)]

        def conv_job(s, w, rows=rows, proj=proj):
            cols = slice(s, s + w)
            res = proj(h, off_xbc + s, w)
            tail = xpad[:, cols]
            conv = cb_ref[:, cols] + cw_ref[B_CONV - 1:B_CONV, cols] * res
            for k in range(1, B_CONV):
                rolled = pltpu.roll(res, k, 0)
                head = jnp.where(sub < k, pltpu.roll(tail, k, 0), rolled[0:CONV_PAD])
                prev = jnp.concatenate([head, rolled[CONV_PAD:]], axis=0)
                conv = conv + cw_ref[B_CONV - 1 - k:B_CONV - k, cols] * prev
            xc_ref[rows, cols] = _silu(conv).astype(BF16)
            xpad[:, cols] = res[PERM_TILE - CONV_PAD:, :]

        heavy = [functools.partial(conv_job, s, w) for s, w in segments(B_CONV_CH)]
        medium = (project_grouped(ga_ref, 3 * A_WIDTH, _silu) + project_to(gb_ref, off_zb, B_WIDTH, _silu)
                  + project_to(gc_ref, off_qc + C_WIDTH, C_WIDTH, _silu))
        light = (project_grouped(ka_ref, A_WIDTH, plain) + project_grouped(va_ref, 2 * A_WIDTH, plain)
                 + project_to(kvc_ref, off_kv, 2 * C_KV_WIDTH, plain) + project_to(dt_ref, off_dt, DT_PAD, plain)
                 + project_grouped(qa_ref, 0, scaled) + project_to(qc_ref, off_qc, C_WIDTH, scaled))
        order = []
        for i, job in enumerate(heavy):
            order += [job, light[i]]
        rest = light[len(heavy):]
        for i, job in enumerate(medium):
            order.append(job)
            if i < len(rest):
                order.append(rest[i])
        for job in order:
            job()


CONV_PAD = 8


def _inproj(x, mod_i, pre_w, w_main, w_tail, layer, perm, conv_w, conv_b):
    bsz, seq, _ = x.shape
    jn = seq // RES
    nat_spec = lambda width: pl.BlockSpec((None, ROW_TILE, width), lambda b, i: (b, i, 0))
    grp_spec = pl.BlockSpec((None, RES, BF16_ROWS * PERM_PER_TILE, A_WIDTH), lambda b, i: (b, 0, i, 0))
    nat = lambda width, dt: jax.ShapeDtypeStruct((bsz, seq, width), dt)
    grp = jax.ShapeDtypeStruct((bsz, RES, jn, A_WIDTH), BF16)
    return pl.pallas_call(
        _inproj_kernel,
        out_shape=[grp, grp, grp, grp,
                   nat(B_WIDTH, BF16), nat(B_CONV_CH, BF16), nat(DT_PAD, F32),
                   nat(C_WIDTH, BF16), nat(C_WIDTH, BF16), nat(2 * C_KV_WIDTH, BF16)],
        grid=(bsz, seq // ROW_TILE),
        in_specs=[
            nat_spec(D_MODEL),
            pl.BlockSpec((None, 1, 3 * D_MODEL), lambda b, i: (b, 0, 0)),
            pl.BlockSpec((1, D_MODEL), lambda b, i: (0, 0)),
            pl.BlockSpec((None, D_MODEL, IN_COLS_MAIN), lambda b, i: (layer, 0, 0),
                         pipeline_mode=pl.Buffered(1)),
            pl.BlockSpec((None, D_MODEL, IN_COLS_TAIL), lambda b, i: (layer, 0, 0),
                         pipeline_mode=pl.Buffered(1)),
            pl.BlockSpec((PERM_TILE, PERM_TILE), lambda b, i: (0, 0)),
            pl.BlockSpec(conv_w.shape, lambda b, i: (0, 0)),
            pl.BlockSpec(conv_b.shape, lambda b, i: (0, 0)),
        ],
        out_specs=[grp_spec, grp_spec, grp_spec, grp_spec,
                   nat_spec(B_WIDTH), nat_spec(B_CONV_CH), nat_spec(DT_PAD),
                   nat_spec(C_WIDTH), nat_spec(C_WIDTH), nat_spec(2 * C_KV_WIDTH)],
        scratch_shapes=[pltpu.VMEM((CONV_PAD, B_CONV_CH), F32),
                        pltpu.VMEM((ROW_TILE, D_MODEL), BF16), pltpu.VMEM((ROW_TILE, D_MODEL), BF16),
                        pltpu.VMEM((D_MODEL, IN_COLS_MAIN), BF16)],
        compiler_params=pltpu.CompilerParams(dimension_semantics=("arbitrary", "arbitrary"),
                                             vmem_limit_bytes=VMEM_LIMIT),
        name="in_proj",
    )(x, mod_i, pre_w, w_main, w_tail, perm, conv_w, conv_b)


def _head_masks():
    lane = lax.broadcasted_iota(jnp.int32, (BLK, LANES), 1)
    lo = lane < HEAD_DIM
    return lo, lo.astype(F32).astype(BF16), (~lo).astype(F32).astype(BF16)


def _attend_pairs(blocks, lo, mlo, mhi):
    scores = []
    for qb, kb, _, bias2 in blocks:
        qs = jnp.concatenate([qb * mlo, qb * mhi], axis=0)
        scores.append(lax.dot_general(qs, kb, (((1,), (1,)), ((), ())),
                                      preferred_element_type=F32) + bias2)
    outs = []
    for (_, _, vb, _), s in zip(blocks, scores):
        m = jnp.max(s, axis=-1, keepdims=True)
        p = jnp.exp2(s - m).astype(BF16)
        v_ext = jnp.concatenate([vb, jnp.ones_like(vb)], axis=1)
        pv = jnp.dot(p, v_ext, preferred_element_type=F32)
        outs.append((jnp.where(lo, m[:BLK], m[BLK:]),
                     jnp.where(lo, pv[:BLK, LANES:], pv[BLK:, LANES:]),
                     jnp.where(lo, pv[:BLK, :LANES], pv[BLK:, :LANES])))
    return outs


def _band_bias(qpos, kpos, window):
    dist = qpos[:, None] - kpos[None, :]
    ok = (dist >= 0) & (dist <= window)
    b = np.where(ok, 0.0, NEG).astype(np.float32)
    return np.concatenate([b, b], axis=0)


def _dilated_biases():
    first, later = [], []
    r, jj = np.meshgrid(np.arange(RES), np.arange(8), indexing="ij")
    qpos = (16 * jj + r).reshape(-1)
    r, jj = np.meshgrid(np.arange(RES), np.arange(16), indexing="ij")
    kpos = (16 * (jj - 8) + r).reshape(-1)
    first.append(_band_bias(qpos, qpos, A_WINDOW))
    later.append(_band_bias(qpos, kpos, A_WINDOW))
    rh, jj = np.meshgrid(np.arange(4), np.arange(32), indexing="ij")
    qpos = (4 * jj + rh).reshape(-1)
    rh, jj = np.meshgrid(np.arange(4), np.arange(64), indexing="ij")
    kpos = (4 * (jj - 32) + rh).reshape(-1)
    first.append(_band_bias(qpos, qpos, A_WINDOW))
    later.append(_band_bias(qpos, kpos, A_WINDOW))
    qpos = np.arange(BLK)
    kpos = np.arange(2 * BLK) - BLK
    first.append(_band_bias(qpos, qpos, A_WINDOW))
    later.append(_band_bias(qpos, kpos, A_WINDOW))
    return jnp.asarray(np.stack(first)), jnp.asarray(np.stack(later))


def _dilated_kernel(q_ref, k_ref, v_ref, g_ref, b1_ref, b2_ref, o_ref,
                    qf, kf, vf, m1, l1, a1, m2, l2, a2, m3, l3, a3):
    jn = q_ref.shape[1]
    lo, mlo, mhi = _head_masks()
    attend = functools.partial(_attend_pairs, lo=lo, mlo=mlo, mhi=mhi)

    def p3_blocks(r):
        blocks = [(q_ref[r, 0:BLK, :], k_ref[r, 0:BLK, :], v_ref[r, 0:BLK, :], b1_ref[2])]
        for jb in range(1, jn // BLK):
            s0 = (jb - 1) * BLK
            blocks.append((q_ref[r, s0 + BLK:s0 + 2 * BLK, :], k_ref[r, s0:s0 + 2 * BLK, :],
                           v_ref[r, s0:s0 + 2 * BLK, :], b2_ref[2]))
        return blocks

    def p3_store(r, outs):
        for jb, (mm, ll, acc) in enumerate(outs):
            m3[r, jb * BLK:(jb + 1) * BLK, :] = mm
            l3[r, jb * BLK:(jb + 1) * BLK, :] = ll
            a3[r, jb * BLK:(jb + 1) * BLK, :] = acc

    per3 = jn // BLK
    res3 = UNROLL // per3

    def p3_body(rr, carry):
        rs = [rr * res3 + d for d in range(res3)]
        outs = attend([blk for r in rs for blk in p3_blocks(r)])
        for d, r in enumerate(rs):
            p3_store(r, outs[d * per3:(d + 1) * per3])
        return carry
    lax.fori_loop(0, RES // res3, p3_body, 0)

    def p2_rows(ref, r4, start, size):
        return jnp.concatenate([ref[4 * rh + r4, pl.ds(start, size), :] for rh in range(4)], axis=0)

    def p2_store(r4, start, out):
        mm, ll, acc = out
        for rh in range(4):
            sl = slice(32 * rh, 32 * rh + 32)
            m2[4 * rh + r4, pl.ds(start, 32), :] = mm[sl]
            l2[4 * rh + r4, pl.ds(start, 32), :] = ll[sl]
            a2[4 * rh + r4, pl.ds(start, 32), :] = acc[sl]

    def p2_first(r4):
        return (p2_rows(q_ref, r4, 0, 32), p2_rows(k_ref, r4, 0, 32), p2_rows(v_ref, r4, 0, 32), b1_ref[1])

    def p2_later(r4, q0):
        k0 = q0 - 32 if isinstance(q0, int) else pl.multiple_of(q0 - 32, 32)
        return (p2_rows(q_ref, r4, q0, 32), p2_rows(k_ref, r4, k0, 64), p2_rows(v_ref, r4, k0, 64), b2_ref[1])

    steps2 = UNROLL // 4
    outs = attend([p2_first(r4) for r4 in range(4)]
                  + [p2_later(r4, 32 * b4) for b4 in range(1, steps2) for r4 in range(4)])
    for b4 in range(steps2):
        for r4 in range(4):
            p2_store(r4, 32 * b4, outs[4 * b4 + r4])

    def p2_body(i, carry):
        starts = [pl.multiple_of((i * steps2 + d) * 32, 32) for d in range(steps2)]
        outs = attend([p2_later(r4, q0) for q0 in starts for r4 in range(4)])
        for d, q0 in enumerate(starts):
            for r4 in range(4):
                p2_store(r4, q0, outs[4 * d + r4])
        return carry
    lax.fori_loop(1, jn // (32 * steps2), p2_body, 0)

    for r in range(RES):
        qf[r] = q_ref[r].astype(F32)
        kf[r] = k_ref[r].astype(F32)
        vf[r] = v_ref[r].astype(F32)

    def p1_rows(ref, start, size):
        return jnp.concatenate([ref[r, pl.ds(start, size), :] for r in range(RES)], axis=0).astype(BF16)

    def p1_store(start, out):
        mm, ll, acc = out
        for r in range(RES):
            sl = slice(8 * r, 8 * r + 8)
            m1[r, pl.ds(start, 8), :] = mm[sl]
            l1[r, pl.ds(start, 8), :] = ll[sl]
            a1[r, pl.ds(start, 8), :] = acc[sl]

    def p1_later(q0):
        k0 = q0 - 8 if isinstance(q0, int) else pl.multiple_of(q0 - 8, 8)
        return (p1_rows(qf, q0, 8), p1_rows(kf, k0, 16), p1_rows(vf, k0, 16), b2_ref[0])

    outs = attend([(p1_rows(qf, 0, 8), p1_rows(kf, 0, 8), p1_rows(vf, 0, 8), b1_ref[0])]
                  + [p1_later(8 * a) for a in range(1, UNROLL)])
    for a in range(UNROLL):
        p1_store(8 * a, outs[a])

    def p1_body(i, carry):
        starts = [pl.multiple_of((i * UNROLL + d) * 8, 8) for d in range(UNROLL)]
        outs = attend([p1_later(q0) for q0 in starts])
        for q0, out in zip(starts, outs):
            p1_store(q0, out)
        return carry
    lax.fori_loop(1, jn // (8 * UNROLL), p1_body, 0)

    def merge_body(r, carry):
        ma, mb, mc = m1[r], m2[r], m3[r]
        mx = jnp.maximum(jnp.maximum(ma, mb), mc)
        ea, eb, ec = jnp.exp2(ma - mx), jnp.exp2(mb - mx), jnp.exp2(mc - mx)
        num = ea * a1[r] + eb * a2[r] + ec * a3[r]
        den = ea * l1[r] + eb * l2[r] + ec * l3[r]
        o_ref[r] = (num / den * g_ref[r].astype(F32)).astype(BF16)
        return carry
    lax.fori_loop(0, RES, merge_body, 0)


def _dilated_attention(qa, ka, va, ga, bias_first, bias_later):
    bsz, _, jn, _ = qa.shape
    assert jn % (8 * UNROLL) == 0 and jn % BLK == 0
    spec = pl.BlockSpec((None, RES, jn, LANES), lambda b, hp: (b, 0, 0, hp))
    acc = pltpu.VMEM((RES, jn, LANES), F32)
    return pl.pallas_call(
        _dilated_kernel,
        out_shape=jax.ShapeDtypeStruct(qa.shape, BF16),
        grid=(bsz, A_WIDTH // LANES),
        in_specs=[spec, spec, spec, spec,
                  pl.BlockSpec(bias_first.shape, lambda b, hp: (0, 0, 0)),
                  pl.BlockSpec(bias_later.shape, lambda b, hp: (0, 0, 0))],
        out_specs=spec,
        scratch_shapes=[acc] * 12,
        compiler_params=pltpu.CompilerParams(dimension_semantics=("arbitrary", "arbitrary"),
                                             vmem_limit_bytes=VMEM_LIMIT),
        name="dilated_attn",
    )(qa, ka, va, ga, bias_first, bias_later)


C_TILE = 1024


def _sink_kernel(q_ref, g_ref, kv_ref, kvp_ref, sink_ref, bias_ref, o_ref):
    first_tile = pl.program_id(1) == 0
    lo, mlo, mhi = _head_masks()
    ngrp = C_WIDTH // LANES
    per_call = UNROLL // ngrp
    for u0 in range(0, C_TILE // BLK, per_call):
        blocks = []
        for u in range(u0, u0 + per_call):
            r0 = u * BLK
            cur = kv_ref[r0:r0 + BLK, :]
            if u == 0:
                prev = kvp_ref[...]
                bias = bias_ref[jnp.where(first_tile, 0, 1)]
            else:
                prev = kv_ref[r0 - BLK:r0, :]
                bias = bias_ref[1]
            kb = jnp.concatenate([prev[:, :LANES], cur[:, :LANES]], axis=0)
            vb = jnp.concatenate([prev[:, LANES:], cur[:, LANES:]], axis=0)
            for g in range(ngrp):
                blocks.append((q_ref[r0:r0 + BLK, g * LANES:(g + 1) * LANES], kb, vb, bias))
        outs = _attend_pairs(blocks, lo, mlo, mhi)
        for idx, (mm, ll, acc) in enumerate(outs):
            r0 = (u0 + idx // ngrp) * BLK
            cols = slice((idx % ngrp) * LANES, (idx % ngrp + 1) * LANES)
            sk = sink_ref[idx % ngrp]
            m2 = jnp.maximum(mm, sk)
            e = jnp.exp2(mm - m2)
            den = ll * e + jnp.exp2(sk - m2)
            gate = g_ref[r0:r0 + BLK, cols].astype(F32)
            o_ref[r0:r0 + BLK, cols] = (acc * e / den * gate).astype(BF16)


def _sink_attention(qc, gc, kvc, sink_lanes, bias):
    bsz, seq, _ = qc.shape
    per = C_TILE // BLK
    return pl.pallas_call(
        _sink_kernel,
        out_shape=jax.ShapeDtypeStruct(qc.shape, BF16),
        grid=(bsz, seq // C_TILE),
        in_specs=[
            pl.BlockSpec((None, C_TILE, C_WIDTH), lambda b, i: (b, i, 0)),
            pl.BlockSpec((None, C_TILE, C_WIDTH), lambda b, i: (b, i, 0)),
            pl.BlockSpec((None, C_TILE, 2 * C_KV_WIDTH), lambda b, i: (b, i, 0)),
            pl.BlockSpec((None, BLK, 2 * C_KV_WIDTH), lambda b, i: (b, jnp.maximum(i * per - 1, 0), 0)),
            pl.BlockSpec(sink_lanes.shape, lambda b, i: (0, 0, 0)),
            pl.BlockSpec(bias.shape, lambda b, i: (0, 0, 0)),
        ],
        out_specs=pl.BlockSpec((None, C_TILE, C_WIDTH), lambda b, i: (b, i, 0)),
        compiler_params=pltpu.CompilerParams(dimension_semantics=("arbitrary", "arbitrary"),
                                             vmem_limit_bytes=VMEM_LIMIT),
        name="sink_attn",
    )(qc, gc, kvc, kvc, sink_lanes, bias)


SSD_TILE = 1024
SSD_INTERLEAVE = 4


def _ssd_kernel(xc_ref, gz_ref, dt_ref, dtb_ref, alog_ref, dexp_ref, nw_ref,
                expand_ref, tri_ref, o_ref, state):
    q = SSD_CHUNK
    hg = B_HEADS // B_GROUPS
    gw = B_WIDTH // B_GROUPS

    @pl.when(pl.program_id(1) == 0)
    def _():
        state[...] = jnp.zeros_like(state)

    tri = tri_ref[...]
    causal = tri > 0
    expand = expand_ref[...]
    lo = lax.broadcasted_iota(jnp.int32, (q, LANES), 1) < HEAD_DIM
    a2 = -jnp.exp(alog_ref[...]) * LOG2E

    gcols = [slice(g * gw, (g + 1) * gw) for g in range(B_GROUPS)]
    st = [state[:, gc] for gc in gcols]

    for c_lo in range(0, SSD_TILE // q, SSD_INTERLEAVE):
        chunks = [dict(t0=c * q) for c in range(c_lo, c_lo + SSD_INTERLEAVE)]
        for ch in chunks:
            rows = slice(ch["t0"], ch["t0"] + q)
            ch["xs_b"] = xc_ref[rows, 0:B_WIDTH]
            dt = jax.nn.softplus(dt_ref[rows, :] + dtb_ref[...])
            da = dt * a2
            a_cum = _split_dot_left(tri, da)
            ch.update(dt=dt, a_cum=a_cum, a_cum_t=a_cum.T, dt_t=dt.T)
        for ch in chunks:
            a_cum = ch["a_cum"]
            dte = jnp.exp2(a_cum[q - 1:q, :] - a_cum)
            ch["ea_x"] = _split_dot(jnp.exp2(a_cum), expand)
            w_x = jnp.dot((ch["dt"] * dte).astype(BF16), expand, preferred_element_type=F32)
            ch["xw_b"] = (ch["xs_b"].astype(F32) * w_x).astype(BF16)
        for ch in chunks:
            rows = slice(ch["t0"], ch["t0"] + q)
            ch["cm_b"], ch["cbm"], ch["new"] = [], [], []
            for g in range(B_GROUPS):
                b0 = B_WIDTH + g * B_STATE
                c0 = B_WIDTH + (B_GROUPS + g) * B_STATE
                bm_b = xc_ref[rows, b0:b0 + B_STATE]
                cm_b = xc_ref[rows, c0:c0 + B_STATE]
                ch["cm_b"].append(cm_b)
                ch["cbm"].append(lax.dot_general(cm_b, bm_b, (((1,), (1,)), ((), ())),
                                                 preferred_element_type=F32))
                bm_t = bm_b.astype(F32).T.astype(BF16)
                ch["new"].append(jnp.dot(bm_t, ch["xw_b"][:, gcols[g]], preferred_element_type=F32))
        for ch in chunks:
            ch["y_off"] = []
            for g in range(B_GROUPS):
                ea_g = ch["ea_x"][:, gcols[g]]
                ch["y_off"].append(jnp.dot(ch["cm_b"][g], st[g].astype(BF16), preferred_element_type=F32) * ea_g)
                st[g] = st[g] * ea_g[q - 1:q, :] + ch["new"][g]
        for ch in chunks:
            y_parts = []
            for g in range(B_GROUPS):
                y_diag = []
                for pr in range(hg // 2):
                    h0 = g * hg + 2 * pr
                    mats = []
                    for h in (h0, h0 + 1):
                        seg = ch["a_cum"][:, h:h + 1] - ch["a_cum_t"][h:h + 1, :]
                        decay = jnp.exp2(jnp.where(causal, seg, NEG))
                        mats.append((ch["cbm"][g] * decay * ch["dt_t"][h:h + 1, :]).astype(BF16))
                    lhs = jnp.concatenate(mats, axis=0)
                    x0 = (h0 // 2) * LANES
                    yy = jnp.dot(lhs, ch["xs_b"][:, x0:x0 + LANES], preferred_element_type=F32)
                    y_diag.append(jnp.where(lo, yy[:q], yy[q:]))
                y_parts.append(jnp.concatenate(y_diag, axis=1) + ch["y_off"][g])
            ch["y"] = jnp.concatenate(y_parts, axis=1)
        for ch in chunks:
            rows = slice(ch["t0"], ch["t0"] + q)
            y = ch["y"] + dexp_ref[...] * ch["xs_b"].astype(F32)
            y = y * gz_ref[rows, :].astype(F32)
            outs = []
            for g in range(B_GROUPS):
                yg = y[:, gcols[g]]
                outs.append(yg * lax.rsqrt(jnp.mean(yg * yg, axis=-1, keepdims=True) + NORM_EPS))
            o_ref[rows, :] = (jnp.concatenate(outs, axis=1) * nw_ref[...]).astype(BF16)

    for g in range(B_GROUPS):
        state[:, gcols[g]] = st[g]


def _split_dot_left(w_bf16, x):
    hi = x.astype(BF16)
    lo = (x - hi.astype(F32)).astype(BF16)
    return (jnp.dot(w_bf16, hi, preferred_element_type=F32)
            + jnp.dot(w_bf16, lo, preferred_element_type=F32))


def _ssd(xc, gzb, dt_raw, dt_bias_row, a_log_row, d_exp, norm_w, expand, tri):
    bsz, seq, _ = xc.shape
    const = lambda a: pl.BlockSpec(a.shape, lambda b, i: (0,) * a.ndim)
    return pl.pallas_call(
        _ssd_kernel,
        out_shape=jax.ShapeDtypeStruct((bsz, seq, B_WIDTH), BF16),
        grid=(bsz, seq // SSD_TILE),
        in_specs=[
            pl.BlockSpec((None, SSD_TILE, B_CONV_CH), lambda b, i: (b, i, 0)),
            pl.BlockSpec((None, SSD_TILE, B_WIDTH), lambda b, i: (b, i, 0)),
            pl.BlockSpec((None, SSD_TILE, DT_PAD), lambda b, i: (b, i, 0)),
            const(dt_bias_row), const(a_log_row), const(d_exp), const(norm_w),
            const(expand), const(tri),
        ],
        out_specs=pl.BlockSpec((None, SSD_TILE, B_WIDTH), lambda b, i: (b, i, 0)),
        scratch_shapes=[pltpu.VMEM((B_STATE, B_WIDTH), F32)],
        compiler_params=pltpu.CompilerParams(dimension_semantics=("arbitrary", "arbitrary"),
                                             vmem_limit_bytes=VMEM_LIMIT),
        name="ssd",
    )(xc, gzb, dt_raw, dt_bias_row, a_log_row, d_exp, norm_w, expand, tri)


OUT_UNITS = 2
OUT_TILE = ROW_TILE * OUT_UNITS


def _outproj_kernel(ya_ref, yb_ref, yc_ref, x_ref, mod_ref, pw_ref, w_ref, permt_ref, o_ref):
    for unit in range(OUT_UNITS):
        rows = slice(unit * ROW_TILE, (unit + 1) * ROW_TILE)
        parts = []
        for u in range(unit * PERM_PER_TILE, (unit + 1) * PERM_PER_TILE):
            ya_res = jnp.concatenate([ya_ref[r, u * BF16_ROWS:(u + 1) * BF16_ROWS, :] for r in range(RES)], axis=0)
            parts.append(jnp.dot(permt_ref[...], ya_res, preferred_element_type=F32).astype(BF16))
        ya = jnp.concatenate(parts, axis=0)
        ycat = jnp.concatenate([ya, yb_ref[rows, :], yc_ref[rows, :]], axis=1)
        y = jnp.dot(ycat, w_ref[...], preferred_element_type=F32)
        ms = jnp.mean(y * y, axis=-1, keepdims=True)
        yn = y * lax.rsqrt(ms + NORM_EPS) * pw_ref[...]
        o_ref[rows, :] = x_ref[rows, :] + mod_ref[:, 2 * D_MODEL:] * yn


def _outproj(ya, yb, yc, x, mod_i, post_w, w_out_packed, layer, perm_t):
    bsz, seq, _ = x.shape
    nat_spec = lambda width: pl.BlockSpec((None, OUT_TILE, width), lambda b, i: (b, i, 0))
    return pl.pallas_call(
        _outproj_kernel,
        out_shape=jax.ShapeDtypeStruct(x.shape, F32),
        grid=(bsz, seq // OUT_TILE),
        in_specs=[
            pl.BlockSpec((None, RES, BF16_ROWS * PERM_PER_TILE * OUT_UNITS, A_WIDTH), lambda b, i: (b, 0, i, 0)),
            nat_spec(B_WIDTH), nat_spec(C_WIDTH), nat_spec(D_MODEL),
            pl.BlockSpec((None, 1, 3 * D_MODEL), lambda b, i: (b, 0, 0)),
            pl.BlockSpec((1, D_MODEL), lambda b, i: (0, 0)),
            pl.BlockSpec((None,) + w_out_packed.shape[1:], lambda b, i: (layer, 0, 0)),
            pl.BlockSpec((PERM_TILE, PERM_TILE), lambda b, i: (0, 0)),
        ],
        out_specs=nat_spec(D_MODEL),
        compiler_params=pltpu.CompilerParams(dimension_semantics=("arbitrary", "arbitrary"),
                                             vmem_limit_bytes=VMEM_LIMIT),
        name="out_proj",
    )(ya, yb, yc, x, mod_i, post_w, w_out_packed, perm_t)


def _c_head_order():
    ngrp = C_WIDTH // LANES
    return [g + ngrp * half for g in range(ngrp) for half in range(2)]


def _reorder_c_heads(w, axis):
    heads = [lax.slice_in_dim(w, h * HEAD_DIM, (h + 1) * HEAD_DIM, axis=axis) for h in _c_head_order()]
    return jnp.concatenate(heads, axis=axis)


def _pack_w_in_tail(w):
    t = w[:, :, IN_COLS_MAIN:].astype(BF16)
    qc = _reorder_c_heads(t[:, :, B_HEADS:B_HEADS + C_WIDTH], 2)
    zc = _reorder_c_heads(t[:, :, B_HEADS + C_WIDTH:B_HEADS + 2 * C_WIDTH], 2)
    kv = t[:, :, B_HEADS + 2 * C_WIDTH:B_HEADS + 2 * C_WIDTH + 2 * C_KV_WIDTH]
    dt = jnp.pad(t[:, :, :B_HEADS], ((0, 0), (0, 0), (0, DT_PAD - B_HEADS)))
    return jnp.concatenate([qc, zc, kv, dt], axis=2)


def _pack_w_out(w):
    c0 = A_WIDTH + B_WIDTH
    w = w.astype(BF16)
    return jnp.concatenate([w[:, :c0], _reorder_c_heads(w[:, c0:], 1)], axis=1)


def kernel(x, c, ada_w, ada_b, pre_norm_w, post_norm_w, w_in, conv_w, conv_b, dt_bias, a_log, d_skip,
           ssm_norm_w, sinks, w_out):
    bsz, seq, _ = x.shape
    assert seq % (RES * BLK) == 0 and seq % SSD_TILE == 0 and seq % C_TILE == 0

    mod = _modulation(c, ada_w, ada_b)
    bias_first, bias_later = _dilated_biases()
    kpos = np.arange(2 * BLK) - BLK
    swa = _band_bias(np.arange(BLK), kpos, C_WINDOW)
    swa_first = np.where(kpos[None, :] >= 0, swa, NEG).astype(np.float32)
    bias_c = jnp.asarray(np.stack([swa_first, swa]))
    expand_np = np.zeros((LANES, B_WIDTH), np.float32)
    for h in range(B_HEADS):
        expand_np[h, h * HEAD_DIM:(h + 1) * HEAD_DIM] = 1.0
    expand = jnp.asarray(expand_np).astype(BF16)
    tri = jnp.asarray(np.tril(np.ones((SSD_CHUNK, SSD_CHUNK), np.float32))).astype(BF16)
    perm_np = _residue_perm()
    perm = jnp.asarray(perm_np).astype(BF16)
    perm_t = jnp.asarray(perm_np.T).astype(BF16)
    pad16 = lambda v: jnp.pad(v.astype(F32), (0, DT_PAD - B_HEADS)).reshape(1, DT_PAD)

    w_in_tail = _pack_w_in_tail(w_in)
    w_out_packed = _pack_w_out(w_out)
    for i in range(DEPTH):
        qa, ka, va, ga, gzb, xc, dt_raw, qc, gc, kvc = _inproj(
            x, mod[i], pre_norm_w[i].reshape(1, D_MODEL), w_in, w_in_tail, i, perm,
            conv_w[i], conv_b[i].reshape(1, B_CONV_CH))

        ya = _dilated_attention(qa, ka, va, ga, bias_first, bias_later)

        yb = _ssd(xc, gzb, dt_raw, pad16(dt_bias[i]), pad16(a_log[i]),
                  jnp.repeat(d_skip[i].astype(F32), HEAD_DIM).reshape(1, B_WIDTH),
                  ssm_norm_w[i].reshape(1, B_WIDTH), expand, tri)

        sink_lanes = jnp.repeat((sinks[i].astype(F32) * LOG2E).reshape(2, C_HEADS // 2).T, HEAD_DIM, axis=1)
        yc = _sink_attention(qc, gc, kvc, sink_lanes.reshape(C_HEADS // 2, 1, LANES), bias_c)

        x = _outproj(ya, yb, yc, x, mod[i], post_norm_w[i].reshape(1, D_MODEL),
                     w_out_packed, i, perm_t)
    return x
```

```python
import functools
import math

import numpy as np
import jax
import jax.numpy as jnp
from jax import lax
from jax.experimental import pallas as pl
from jax.experimental.pallas import tpu as pltpu

F32 = jnp.float32
BF16 = jnp.bfloat16

D_MODEL = 1024
DEPTH = 4
HEAD_DIM = 64
LANES = 128
BF16_ROWS = 16
A_WIDTH = 512
B_WIDTH = 1024
B_HEADS = 16
B_GROUPS = 2
B_STATE = 128
B_CONV = 4
B_CONV_CH = B_WIDTH + 2 * B_GROUPS * B_STATE
SSD_CHUNK = 128
C_WIDTH = 512
C_HEADS = 8
C_KV_WIDTH = 128
C_WINDOW = 128
A_WINDOW = 128
NORM_EPS = 1e-6
RES = 16
BLK = 128
NEG = -1e30
LOG2E = math.log2(math.e)
DT_PAD = 128
IN_COLS_MAIN = 4 * A_WIDTH + B_WIDTH + B_CONV_CH
IN_COLS_TAIL = 2 * C_WIDTH + 2 * C_KV_WIDTH + DT_PAD
PERM_TILE = RES * BF16_ROWS
PERM_PER_TILE = 2
ROW_TILE = PERM_TILE * PERM_PER_TILE
SEG_COLS = 256
VMEM_LIMIT = 56 * 1024 * 1024
UNROLL = 16


def _silu(v):
    return v * jax.nn.sigmoid(v)


def _split_dot(x, w_bf16):
    hi = x.astype(BF16)
    lo = (x - hi.astype(F32)).astype(BF16)
    return (jnp.dot(hi, w_bf16, preferred_element_type=F32)
            + jnp.dot(lo, w_bf16, preferred_element_type=F32))


def _residue_perm():
    p = np.zeros((PERM_TILE, PERM_TILE), np.float32)
    for r in range(RES):
        for jj in range(BF16_ROWS):
            p[r * BF16_ROWS + jj, RES * jj + r] = 1.0
    return p


def _mod_kernel(c_ref, w_ref, b_ref, o_ref):
    ca = _silu(c_ref[...])
    o_ref[...] = jnp.dot(ca, w_ref[...], preferred_element_type=F32,
                         precision=lax.Precision.HIGHEST) + b_ref[...]


def _modulation(c, ada_w, ada_b):
    bsz = c.shape[0]
    ada_b4 = ada_b.reshape(DEPTH, 3, 1, D_MODEL)
    return pl.pallas_call(
        _mod_kernel,
        out_shape=jax.ShapeDtypeStruct((DEPTH, bsz, 3 * D_MODEL), F32),
        grid=(DEPTH, 3),
        in_specs=[
            pl.BlockSpec((bsz, D_MODEL), lambda i, k: (0, 0)),
            pl.BlockSpec((None, D_MODEL, D_MODEL), lambda i, k: (i, 0, k)),
            pl.BlockSpec((None, None, 1, D_MODEL), lambda i, k: (i, k, 0, 0)),
        ],
        out_specs=pl.BlockSpec((None, bsz, D_MODEL), lambda i, k: (i, 0, k)),
        compiler_params=pltpu.CompilerParams(dimension_semantics=("arbitrary", "arbitrary")),
        name="ada_mod",
    )(c, ada_w, ada_b4).reshape(DEPTH, bsz, 1, 3 * D_MODEL)


def _inproj_kernel(x_ref, mod_ref, pw_ref, w32_ref, wt_ref, perm_ref, cw_ref, cb_ref,
                   qa_ref, ka_ref, va_ref, ga_ref, gb_ref, xc_ref, dt_ref,
                   qc_ref, gc_ref, kvc_ref, xpad, h, h_res, w_ref):
    @pl.when((pl.program_id(0) == 0) & (pl.program_id(1) == 0))
    def _():
        for c0 in range(0, IN_COLS_MAIN, 2 * SEG_COLS):
            w_ref[c0:c0 + 2 * SEG_COLS, :] = w32_ref[c0:c0 + 2 * SEG_COLS, :].astype(BF16)

    @pl.when(pl.program_id(1) == 0)
    def _():
        xpad[...] = jnp.zeros_like(xpad)

    qscale = HEAD_DIM ** -0.5 * LOG2E
    off_zb = 4 * A_WIDTH
    off_xbc = off_zb + B_WIDTH
    off_qc = off_xbc + B_CONV_CH
    off_kv = off_qc + 2 * C_WIDTH
    off_dt = off_kv + 2 * C_KV_WIDTH
    shift, scale_mod = mod_ref[:, 0:D_MODEL], mod_ref[:, D_MODEL:2 * D_MODEL]
    sub = lax.broadcasted_iota(jnp.int32, (CONV_PAD, SEG_COLS), 0)
    scaled = lambda v: v * qscale
    plain = lambda v: v

    def segments(width):
        return [(s, min(SEG_COLS, width - s)) for s in range(0, width, SEG_COLS)]

    for u in range(PERM_PER_TILE):
        rows = slice(u * PERM_TILE, (u + 1) * PERM_TILE)
        x = x_ref[rows, :]
        ms = jnp.mean(x * x, axis=-1, keepdims=True)
        xn = x * lax.rsqrt(ms + NORM_EPS) * pw_ref[...]
        h[rows, :] = (xn * (1.0 + scale_mod) + shift).astype(BF16)
        h_res[rows, :] = jnp.dot(perm_ref[...], h[rows, :], preferred_element_type=F32).astype(BF16)

        def proj(lhs, c0, width, rows=rows):
            if c0 >= IN_COLS_MAIN:
                rhs = wt_ref[c0 - IN_COLS_MAIN:c0 - IN_COLS_MAIN + width, :]
            else:
                rhs = w_ref[c0:c0 + width, :]
            return lax.dot_general(lhs[rows, :], rhs, (((1,), (1,)), ((), ())), preferred_element_type=F32)

        def project_to(ref, c0, width, post, rows=rows, proj=proj):
            def job(s, w):
                ref[rows, s:s + w] = post(proj(h, c0 + s, w)).astype(ref.dtype)
            return [functools.partial(job, s, w) for s, w in segments(width)]

        def project_grouped(ref, c0, post, u=u, proj=proj):
            def job(s, w):
                val = post(proj(h_res, c0 + s, w)).astype(BF16)
                for r in range(RES):
                    ref[r, u * BF16_ROWS:(u + 1) * BF16_ROWS, s:s + w] = val[r * BF16_ROWS:(r + 1) * BF16_ROWS]
            return [functools.partial(job, s, w) for s, w in segments(A_WIDTH)]

        def conv_job(s, w, rows=rows, proj=proj):
            cols = slice(s, s + w)
            res = proj(h, off_xbc + s, w)
            tail = xpad[:, cols]
            conv = cb_ref[:, cols] + cw_ref[B_CONV - 1:B_CONV, cols] * res
            for k in range(1, B_CONV):
                rolled = pltpu.roll(res, k, 0)
                head = jnp.where(sub < k, pltpu.roll(tail, k, 0), rolled[0:CONV_PAD])
                prev = jnp.concatenate([head, rolled[CONV_PAD:]], axis=0)
                conv = conv + cw_ref[B_CONV - 1 - k:B_CONV - k, cols] * prev
            xc_ref[rows, cols] = _silu(conv).astype(BF16)
            xpad[:, cols] = res[PERM_TILE - CONV_PAD:, :]

        heavy = [functools.partial(conv_job, s, w) for s, w in segments(B_CONV_CH)]
        medium = (project_grouped(ga_ref, 3 * A_WIDTH, _silu) + project_to(gb_ref, off_zb, B_WIDTH, _silu)
                  + project_to(gc_ref, off_qc + C_WIDTH, C_WIDTH, _silu))
        light = (project_grouped(ka_ref, A_WIDTH, plain) + project_grouped(va_ref, 2 * A_WIDTH, plain)
                 + project_to(kvc_ref, off_kv, 2 * C_KV_WIDTH, plain) + project_to(dt_ref, off_dt, DT_PAD, plain)
                 + project_grouped(qa_ref, 0, scaled) + project_to(qc_ref, off_qc, C_WIDTH, scaled))
        order = []
        for i, job in enumerate(heavy):
            order += [job, light[i]]
        rest = light[len(heavy):]
        for i, job in enumerate(medium):
            order.append(job)
            if i < len(rest):
                order.append(rest[i])
        for job in order:
            job()


CONV_PAD = 8


def _inproj(x, mod_i, pre_w, w_main, w_tail, layer, perm, conv_w, conv_b):
    bsz, seq, _ = x.shape
    jn = seq // RES
    nat_spec = lambda width: pl.BlockSpec((None, ROW_TILE, width), lambda b, i: (b, i, 0))
    grp_spec = pl.BlockSpec((None, RES, BF16_ROWS * PERM_PER_TILE, A_WIDTH), lambda b, i: (b, 0, i, 0))
    nat = lambda width, dt: jax.ShapeDtypeStruct((bsz, seq, width), dt)
    grp = jax.ShapeDtypeStruct((bsz, RES, jn, A_WIDTH), BF16)
    return pl.pallas_call(
        _inproj_kernel,
        out_shape=[grp, grp, grp, grp,
                   nat(B_WIDTH, BF16), nat(B_CONV_CH, BF16), nat(DT_PAD, F32),
                   nat(C_WIDTH, BF16), nat(C_WIDTH, BF16), nat(2 * C_KV_WIDTH, BF16)],
        grid=(bsz, seq // ROW_TILE),
        in_specs=[
            nat_spec(D_MODEL),
            pl.BlockSpec((None, 1, 3 * D_MODEL), lambda b, i: (b, 0, 0)),
            pl.BlockSpec((1, D_MODEL), lambda b, i: (0, 0)),
            pl.BlockSpec((None, IN_COLS_MAIN, D_MODEL), lambda b, i: (layer, 0, 0),
                         pipeline_mode=pl.Buffered(1)),
            pl.BlockSpec((None, IN_COLS_TAIL, D_MODEL), lambda b, i: (layer, 0, 0),
                         pipeline_mode=pl.Buffered(1)),
            pl.BlockSpec((PERM_TILE, PERM_TILE), lambda b, i: (0, 0)),
            pl.BlockSpec(conv_w.shape, lambda b, i: (0, 0)),
            pl.BlockSpec(conv_b.shape, lambda b, i: (0, 0)),
        ],
        out_specs=[grp_spec, grp_spec, grp_spec, grp_spec,
                   nat_spec(B_WIDTH), nat_spec(B_CONV_CH), nat_spec(DT_PAD),
                   nat_spec(C_WIDTH), nat_spec(C_WIDTH), nat_spec(2 * C_KV_WIDTH)],
        scratch_shapes=[pltpu.VMEM((CONV_PAD, B_CONV_CH), F32),
                        pltpu.VMEM((ROW_TILE, D_MODEL), BF16), pltpu.VMEM((ROW_TILE, D_MODEL), BF16),
                        pltpu.VMEM((IN_COLS_MAIN, D_MODEL), BF16)],
        compiler_params=pltpu.CompilerParams(dimension_semantics=("arbitrary", "arbitrary"),
                                             vmem_limit_bytes=VMEM_LIMIT),
        name="in_proj",
    )(x, mod_i, pre_w, w_main, w_tail, perm, conv_w, conv_b)


def _head_masks():
    lane = lax.broadcasted_iota(jnp.int32, (BLK, LANES), 1)
    lo = lane < HEAD_DIM
    return lo, lo.astype(F32).astype(BF16), (~lo).astype(F32).astype(BF16)


def _attend_pairs(blocks, lo, mlo, mhi):
    scores = []
    for qb, kb, _, bias2 in blocks:
        qs = jnp.concatenate([qb * mlo, qb * mhi], axis=0)
        scores.append(lax.dot_general(qs, kb, (((1,), (1,)), ((), ())),
                                      preferred_element_type=F32) + bias2)
    outs = []
    for (_, _, vb, _), s in zip(blocks, scores):
        m = jnp.max(s, axis=-1, keepdims=True)
        p = jnp.exp2(s - m).astype(BF16)
        v_ext = jnp.concatenate([vb, jnp.ones_like(vb)], axis=1)
        pv = jnp.dot(p, v_ext, preferred_element_type=F32)
        outs.append((jnp.where(lo, m[:BLK], m[BLK:]),
                     jnp.where(lo, pv[:BLK, LANES:], pv[BLK:, LANES:]),
                     jnp.where(lo, pv[:BLK, :LANES], pv[BLK:, :LANES])))
    return outs


def _band_bias(qpos, kpos, window):
    dist = qpos[:, None] - kpos[None, :]
    ok = (dist >= 0) & (dist <= window)
    b = np.where(ok, 0.0, NEG).astype(np.float32)
    return np.concatenate([b, b], axis=0)


def _dilated_biases():
    first, later = [], []
    r, jj = np.meshgrid(np.arange(RES), np.arange(8), indexing="ij")
    qpos = (16 * jj + r).reshape(-1)
    r, jj = np.meshgrid(np.arange(RES), np.arange(16), indexing="ij")
    kpos = (16 * (jj - 8) + r).reshape(-1)
    first.append(_band_bias(qpos, qpos, A_WINDOW))
    later.append(_band_bias(qpos, kpos, A_WINDOW))
    rh, jj = np.meshgrid(np.arange(4), np.arange(32), indexing="ij")
    qpos = (4 * jj + rh).reshape(-1)
    rh, jj = np.meshgrid(np.arange(4), np.arange(64), indexing="ij")
    kpos = (4 * (jj - 32) + rh).reshape(-1)
    first.append(_band_bias(qpos, qpos, A_WINDOW))
    later.append(_band_bias(qpos, kpos, A_WINDOW))
    qpos = np.arange(BLK)
    kpos = np.arange(2 * BLK) - BLK
    first.append(_band_bias(qpos, qpos, A_WINDOW))
    later.append(_band_bias(qpos, kpos, A_WINDOW))
    return jnp.asarray(np.stack(first)), jnp.asarray(np.stack(later))


def _dilated_kernel(q_ref, k_ref, v_ref, g_ref, b1_ref, b2_ref, o_ref,
                    qf, kf, vf, m1, l1, a1, m2, l2, a2, m3, l3, a3):
    jn = q_ref.shape[1]
    lo, mlo, mhi = _head_masks()
    attend = functools.partial(_attend_pairs, lo=lo, mlo=mlo, mhi=mhi)

    def p3_blocks(r):
        blocks = [(q_ref[r, 0:BLK, :], k_ref[r, 0:BLK, :], v_ref[r, 0:BLK, :], b1_ref[2])]
        for jb in range(1, jn // BLK):
            s0 = (jb - 1) * BLK
            blocks.append((q_ref[r, s0 + BLK:s0 + 2 * BLK, :], k_ref[r, s0:s0 + 2 * BLK, :],
                           v_ref[r, s0:s0 + 2 * BLK, :], b2_ref[2]))
        return blocks

    def p3_store(r, outs):
        for jb, (mm, ll, acc) in enumerate(outs):
            m3[r, jb * BLK:(jb + 1) * BLK, :] = mm
            l3[r, jb * BLK:(jb + 1) * BLK, :] = ll
            a3[r, jb * BLK:(jb + 1) * BLK, :] = acc

    per3 = jn // BLK
    res3 = UNROLL // per3

    def p3_body(rr, carry):
        rs = [rr * res3 + d for d in range(res3)]
        outs = attend([blk for r in rs for blk in p3_blocks(r)])
        for d, r in enumerate(rs):
            p3_store(r, outs[d * per3:(d + 1) * per3])
        return carry
    lax.fori_loop(0, RES // res3, p3_body, 0)

    def p2_rows(ref, r4, start, size):
        return jnp.concatenate([ref[4 * rh + r4, pl.ds(start, size), :] for rh in range(4)], axis=0)

    def p2_store(r4, start, out):
        mm, ll, acc = out
        for rh in range(4):
            sl = slice(32 * rh, 32 * rh + 32)
            m2[4 * rh + r4, pl.ds(start, 32), :] = mm[sl]
            l2[4 * rh + r4, pl.ds(start, 32), :] = ll[sl]
            a2[4 * rh + r4, pl.ds(start, 32), :] = acc[sl]

    def p2_first(r4):
        return (p2_rows(q_ref, r4, 0, 32), p2_rows(k_ref, r4, 0, 32), p2_rows(v_ref, r4, 0, 32), b1_ref[1])

    def p2_later(r4, q0):
        k0 = q0 - 32 if isinstance(q0, int) else pl.multiple_of(q0 - 32, 32)
        return (p2_rows(q_ref, r4, q0, 32), p2_rows(k_ref, r4, k0, 64), p2_rows(v_ref, r4, k0, 64), b2_ref[1])

    steps2 = UNROLL // 4
    outs = attend([p2_first(r4) for r4 in range(4)]
                  + [p2_later(r4, 32 * b4) for b4 in range(1, steps2) for r4 in range(4)])
    for b4 in range(steps2):
        for r4 in range(4):
            p2_store(r4, 32 * b4, outs[4 * b4 + r4])

    def p2_body(i, carry):
        starts = [pl.multiple_of((i * steps2 + d) * 32, 32) for d in range(steps2)]
        outs = attend([p2_later(r4, q0) for q0 in starts for r4 in range(4)])
        for d, q0 in enumerate(starts):
            for r4 in range(4):
                p2_store(r4, q0, outs[4 * d + r4])
        return carry
    lax.fori_loop(1, jn // (32 * steps2), p2_body, 0)

    for r in range(RES):
        qf[r] = q_ref[r].astype(F32)
        kf[r] = k_ref[r].astype(F32)
        vf[r] = v_ref[r].astype(F32)

    def p1_rows(ref, start, size):
        return jnp.concatenate([ref[r, pl.ds(start, size), :] for r in range(RES)], axis=0).astype(BF16)

    def p1_store(start, out):
        mm, ll, acc = out
        for r in range(RES):
            sl = slice(8 * r, 8 * r + 8)
            m1[r, pl.ds(start, 8), :] = mm[sl]
            l1[r, pl.ds(start, 8), :] = ll[sl]
            a1[r, pl.ds(start, 8), :] = acc[sl]

    def p1_later(q0):
        k0 = q0 - 8 if isinstance(q0, int) else pl.multiple_of(q0 - 8, 8)
        return (p1_rows(qf, q0, 8), p1_rows(kf, k0, 16), p1_rows(vf, k0, 16), b2_ref[0])

    outs = attend([(p1_rows(qf, 0, 8), p1_rows(kf, 0, 8), p1_rows(vf, 0, 8), b1_ref[0])]
                  + [p1_later(8 * a) for a in range(1, UNROLL)])
    for a in range(UNROLL):
        p1_store(8 * a, outs[a])

    def p1_body(i, carry):
        starts = [pl.multiple_of((i * UNROLL + d) * 8, 8) for d in range(UNROLL)]
        outs = attend([p1_later(q0) for q0 in starts])
        for q0, out in zip(starts, outs):
            p1_store(q0, out)
        return carry
    lax.fori_loop(1, jn // (8 * UNROLL), p1_body, 0)

    def merge_body(r, carry):
        ma, mb, mc = m1[r], m2[r], m3[r]
        mx = jnp.maximum(jnp.maximum(ma, mb), mc)
        ea, eb, ec = jnp.exp2(ma - mx), jnp.exp2(mb - mx), jnp.exp2(mc - mx)
        num = ea * a1[r] + eb * a2[r] + ec * a3[r]
        den = ea * l1[r] + eb * l2[r] + ec * l3[r]
        o_ref[r] = (num / den * g_ref[r].astype(F32)).astype(BF16)
        return carry
    lax.fori_loop(0, RES, merge_body, 0)


def _dilated_attention(qa, ka, va, ga, bias_first, bias_later):
    bsz, _, jn, _ = qa.shape
    assert jn % (8 * UNROLL) == 0 and jn % BLK == 0
    spec = pl.BlockSpec((None, RES, jn, LANES), lambda b, hp: (b, 0, 0, hp))
    acc = pltpu.VMEM((RES, jn, LANES), F32)
    return pl.pallas_call(
        _dilated_kernel,
        out_shape=jax.ShapeDtypeStruct(qa.shape, BF16),
        grid=(bsz, A_WIDTH // LANES),
        in_specs=[spec, spec, spec, spec,
                  pl.BlockSpec(bias_first.shape, lambda b, hp: (0, 0, 0)),
                  pl.BlockSpec(bias_later.shape, lambda b, hp: (0, 0, 0))],
        out_specs=spec,
        scratch_shapes=[acc] * 12,
        compiler_params=pltpu.CompilerParams(dimension_semantics=("arbitrary", "arbitrary"),
                                             vmem_limit_bytes=VMEM_LIMIT),
        name="dilated_attn",
    )(qa, ka, va, ga, bias_first, bias_later)


C_TILE = 1024


def _sink_kernel(q_ref, g_ref, kv_ref, kvp_ref, sink_ref, bias_ref, o_ref):
    first_tile = pl.program_id(1) == 0
    lo, mlo, mhi = _head_masks()
    ngrp = C_WIDTH // LANES
    per_call = UNROLL // ngrp
    for u0 in range(0, C_TILE // BLK, per_call):
        blocks = []
        for u in range(u0, u0 + per_call):
            r0 = u * BLK
            cur = kv_ref[r0:r0 + BLK, :]
            if u == 0:
                prev = kvp_ref[...]
                bias = bias_ref[jnp.where(first_tile, 0, 1)]
            else:
                prev = kv_ref[r0 - BLK:r0, :]
                bias = bias_ref[1]
            kb = jnp.concatenate([prev[:, :LANES], cur[:, :LANES]], axis=0)
            vb = jnp.concatenate([prev[:, LANES:], cur[:, LANES:]], axis=0)
            for g in range(ngrp):
                blocks.append((q_ref[r0:r0 + BLK, g * LANES:(g + 1) * LANES], kb, vb, bias))
        outs = _attend_pairs(blocks, lo, mlo, mhi)
        for idx, (mm, ll, acc) in enumerate(outs):
            r0 = (u0 + idx // ngrp) * BLK
            cols = slice((idx % ngrp) * LANES, (idx % ngrp + 1) * LANES)
            sk = sink_ref[idx % ngrp]
            m2 = jnp.maximum(mm, sk)
            e = jnp.exp2(mm - m2)
            den = ll * e + jnp.exp2(sk - m2)
            gate = g_ref[r0:r0 + BLK, cols].astype(F32)
            o_ref[r0:r0 + BLK, cols] = (acc * e / den * gate).astype(BF16)


def _sink_attention(qc, gc, kvc, sink_lanes, bias):
    bsz, seq, _ = qc.shape
    per = C_TILE // BLK
    return pl.pallas_call(
        _sink_kernel,
        out_shape=jax.ShapeDtypeStruct(qc.shape, BF16),
        grid=(bsz, seq // C_TILE),
        in_specs=[
            pl.BlockSpec((None, C_TILE, C_WIDTH), lambda b, i: (b, i, 0)),
            pl.BlockSpec((None, C_TILE, C_WIDTH), lambda b, i: (b, i, 0)),
            pl.BlockSpec((None, C_TILE, 2 * C_KV_WIDTH), lambda b, i: (b, i, 0)),
            pl.BlockSpec((None, BLK, 2 * C_KV_WIDTH), lambda b, i: (b, jnp.maximum(i * per - 1, 0), 0)),
            pl.BlockSpec(sink_lanes.shape, lambda b, i: (0, 0, 0)),
            pl.BlockSpec(bias.shape, lambda b, i: (0, 0, 0)),
        ],
        out_specs=pl.BlockSpec((None, C_TILE, C_WIDTH), lambda b, i: (b, i, 0)),
        compiler_params=pltpu.CompilerParams(dimension_semantics=("arbitrary", "arbitrary"),
                                             vmem_limit_bytes=VMEM_LIMIT),
        name="sink_attn",
    )(qc, gc, kvc, kvc, sink_lanes, bias)


SSD_TILE = 1024
SSD_INTERLEAVE = 4


def _ssd_kernel(xc_ref, gz_ref, dt_ref, dtb_ref, alog_ref, dexp_ref, nw_ref,
                expand_ref, tri_ref, o_ref, state):
    q = SSD_CHUNK
    hg = B_HEADS // B_GROUPS
    gw = B_WIDTH // B_GROUPS

    @pl.when(pl.program_id(1) == 0)
    def _():
        state[...] = jnp.zeros_like(state)

    tri = tri_ref[...]
    causal = tri > 0
    expand = expand_ref[...]
    lo = lax.broadcasted_iota(jnp.int32, (q, LANES), 1) < HEAD_DIM
    a2 = -jnp.exp(alog_ref[...]) * LOG2E

    gcols = [slice(g * gw, (g + 1) * gw) for g in range(B_GROUPS)]
    st = [state[:, gc] for gc in gcols]

    for c_lo in range(0, SSD_TILE // q, SSD_INTERLEAVE):
        chunks = [dict(t0=c * q) for c in range(c_lo, c_lo + SSD_INTERLEAVE)]
        for ch in chunks:
            rows = slice(ch["t0"], ch["t0"] + q)
            ch["xs_b"] = xc_ref[rows, 0:B_WIDTH]
            dt = jax.nn.softplus(dt_ref[rows, :] + dtb_ref[...])
            da = dt * a2
            a_cum = _split_dot_left(tri, da)
            ch.update(dt=dt, a_cum=a_cum, a_cum_t=a_cum.T, dt_t=dt.T)
        for ch in chunks:
            a_cum = ch["a_cum"]
            dte = jnp.exp2(a_cum[q - 1:q, :] - a_cum)
            ch["ea_x"] = _split_dot(jnp.exp2(a_cum), expand)
            w_x = jnp.dot((ch["dt"] * dte).astype(BF16), expand, preferred_element_type=F32)
            ch["xw_b"] = (ch["xs_b"].astype(F32) * w_x).astype(BF16)
        for ch in chunks:
            rows = slice(ch["t0"], ch["t0"] + q)
            ch["cm_b"], ch["cbm"], ch["new"] = [], [], []
            for g in range(B_GROUPS):
                b0 = B_WIDTH + g * B_STATE
                c0 = B_WIDTH + (B_GROUPS + g) * B_STATE
                bm_b = xc_ref[rows, b0:b0 + B_STATE]
                cm_b = xc_ref[rows, c0:c0 + B_STATE]
                ch["cm_b"].append(cm_b)
                ch["cbm"].append(lax.dot_general(cm_b, bm_b, (((1,), (1,)), ((), ())),
                                                 preferred_element_type=F32))
                bm_t = bm_b.astype(F32).T.astype(BF16)
                ch["new"].append(jnp.dot(bm_t, ch["xw_b"][:, gcols[g]], preferred_element_type=F32))
        for ch in chunks:
            ch["y_off"] = []
            for g in range(B_GROUPS):
                ea_g = ch["ea_x"][:, gcols[g]]
                ch["y_off"].append(jnp.dot(ch["cm_b"][g], st[g].astype(BF16), preferred_element_type=F32) * ea_g)
                st[g] = st[g] * ea_g[q - 1:q, :] + ch["new"][g]
        for ch in chunks:
            y_parts = []
            for g in range(B_GROUPS):
                y_diag = []
                for pr in range(hg // 2):
                    h0 = g * hg + 2 * pr
                    mats = []
                    for h in (h0, h0 + 1):
                        seg = ch["a_cum"][:, h:h + 1] - ch["a_cum_t"][h:h + 1, :]
                        decay = jnp.exp2(jnp.where(causal, seg, NEG))
                        mats.append((ch["cbm"][g] * decay * ch["dt_t"][h:h + 1, :]).astype(BF16))
                    lhs = jnp.concatenate(mats, axis=0)
                    x0 = (h0 // 2) * LANES
                    yy = jnp.dot(lhs, ch["xs_b"][:, x0:x0 + LANES], preferred_element_type=F32)
                    y_diag.append(jnp.where(lo, yy[:q], yy[q:]))
                y_parts.append(jnp.concatenate(y_diag, axis=1) + ch["y_off"][g])
            ch["y"] = jnp.concatenate(y_parts, axis=1)
        for ch in chunks:
            rows = slice(ch["t0"], ch["t0"] + q)
            y = ch["y"] + dexp_ref[...] * ch["xs_b"].astype(F32)
            y = y * gz_ref[rows, :].astype(F32)
            outs = []
            for g in range(B_GROUPS):
                yg = y[:, gcols[g]]
                outs.append(yg * lax.rsqrt(jnp.mean(yg * yg, axis=-1, keepdims=True) + NORM_EPS))
            o_ref[rows, :] = (jnp.concatenate(outs, axis=1) * nw_ref[...]).astype(BF16)

    for g in range(B_GROUPS):
        state[:, gcols[g]] = st[g]


def _split_dot_left(w_bf16, x):
    hi = x.astype(BF16)
    lo = (x - hi.astype(F32)).astype(BF16)
    return (jnp.dot(w_bf16, hi, preferred_element_type=F32)
            + jnp.dot(w_bf16, lo, preferred_element_type=F32))


def _ssd(xc, gzb, dt_raw, dt_bias_row, a_log_row, d_exp, norm_w, expand, tri):
    bsz, seq, _ = xc.shape
    const = lambda a: pl.BlockSpec(a.shape, lambda b, i: (0,) * a.ndim)
    return pl.pallas_call(
        _ssd_kernel,
        out_shape=jax.ShapeDtypeStruct((bsz, seq, B_WIDTH), BF16),
        grid=(bsz, seq // SSD_TILE),
        in_specs=[
            pl.BlockSpec((None, SSD_TILE, B_CONV_CH), lambda b, i: (b, i, 0)),
            pl.BlockSpec((None, SSD_TILE, B_WIDTH), lambda b, i: (b, i, 0)),
            pl.BlockSpec((None, SSD_TILE, DT_PAD), lambda b, i: (b, i, 0)),
            const(dt_bias_row), const(a_log_row), const(d_exp), const(norm_w),
            const(expand), const(tri),
        ],
        out_specs=pl.BlockSpec((None, SSD_TILE, B_WIDTH), lambda b, i: (b, i, 0)),
        scratch_shapes=[pltpu.VMEM((B_STATE, B_WIDTH), F32)],
        compiler_params=pltpu.CompilerParams(dimension_semantics=("arbitrary", "arbitrary"),
                                             vmem_limit_bytes=VMEM_LIMIT),
        name="ssd",
    )(xc, gzb, dt_raw, dt_bias_row, a_log_row, d_exp, norm_w, expand, tri)


OUT_UNITS = 2
OUT_TILE = ROW_TILE * OUT_UNITS


def _outproj_kernel(ya_ref, yb_ref, yc_ref, x_ref, mod_ref, pw_ref, w_ref, permt_ref, o_ref):
    for unit in range(OUT_UNITS):
        rows = slice(unit * ROW_TILE, (unit + 1) * ROW_TILE)
        parts = []
        for u in range(unit * PERM_PER_TILE, (unit + 1) * PERM_PER_TILE):
            ya_res = jnp.concatenate([ya_ref[r, u * BF16_ROWS:(u + 1) * BF16_ROWS, :] for r in range(RES)], axis=0)
            parts.append(jnp.dot(permt_ref[...], ya_res, preferred_element_type=F32).astype(BF16))
        ya = jnp.concatenate(parts, axis=0)
        ycat = jnp.concatenate([ya, yb_ref[rows, :], yc_ref[rows, :]], axis=1)
        y = jnp.dot(ycat, w_ref[...], preferred_element_type=F32)
        ms = jnp.mean(y * y, axis=-1, keepdims=True)
        yn = y * lax.rsqrt(ms + NORM_EPS) * pw_ref[...]
        o_ref[rows, :] = x_ref[rows, :] + mod_ref[:, 2 * D_MODEL:] * yn


def _outproj(ya, yb, yc, x, mod_i, post_w, w_out_packed, layer, perm_t):
    bsz, seq, _ = x.shape
    nat_spec = lambda width: pl.BlockSpec((None, OUT_TILE, width), lambda b, i: (b, i, 0))
    return pl.pallas_call(
        _outproj_kernel,
        out_shape=jax.ShapeDtypeStruct(x.shape, F32),
        grid=(bsz, seq // OUT_TILE),
        in_specs=[
            pl.BlockSpec((None, RES, BF16_ROWS * PERM_PER_TILE * OUT_UNITS, A_WIDTH), lambda b, i: (b, 0, i, 0)),
            nat_spec(B_WIDTH), nat_spec(C_WIDTH), nat_spec(D_MODEL),
            pl.BlockSpec((None, 1, 3 * D_MODEL), lambda b, i: (b, 0, 0)),
            pl.BlockSpec((1, D_MODEL), lambda b, i: (0, 0)),
            pl.BlockSpec((None,) + w_out_packed.shape[1:], lambda b, i: (layer, 0, 0)),
            pl.BlockSpec((PERM_TILE, PERM_TILE), lambda b, i: (0, 0)),
        ],
        out_specs=nat_spec(D_MODEL),
        compiler_params=pltpu.CompilerParams(dimension_semantics=("arbitrary", "arbitrary"),
                                             vmem_limit_bytes=VMEM_LIMIT),
        name="out_proj",
    )(ya, yb, yc, x, mod_i, post_w, w_out_packed, perm_t)


def _c_head_order():
    ngrp = C_WIDTH // LANES
    return [g + ngrp * half for g in range(ngrp) for half in range(2)]


def _reorder_c_heads(w, axis):
    heads = [lax.slice_in_dim(w, h * HEAD_DIM, (h + 1) * HEAD_DIM, axis=axis) for h in _c_head_order()]
    return jnp.concatenate(heads, axis=axis)


def _pack_w_in_tail(w):
    t = lax.optimization_barrier(w[:, IN_COLS_MAIN:, :]).astype(BF16)
    qc = _reorder_c_heads(t[:, B_HEADS:B_HEADS + C_WIDTH], 1)
    zc = _reorder_c_heads(t[:, B_HEADS + C_WIDTH:B_HEADS + 2 * C_WIDTH], 1)
    kv = t[:, B_HEADS + 2 * C_WIDTH:B_HEADS + 2 * C_WIDTH + 2 * C_KV_WIDTH]
    dt = jnp.pad(t[:, :B_HEADS], ((0, 0), (0, DT_PAD - B_HEADS), (0, 0)))
    return jnp.concatenate([qc, zc, kv, dt], axis=1)


def _pack_w_out(w):
    c0 = A_WIDTH + B_WIDTH
    w = w.astype(BF16)
    return jnp.concatenate([w[:, :c0], _reorder_c_heads(w[:, c0:], 1)], axis=1)


def kernel(x, c, ada_w, ada_b, pre_norm_w, post_norm_w, w_in, conv_w, conv_b, dt_bias, a_log, d_skip,
           ssm_norm_w, sinks, w_out):
    bsz, seq, _ = x.shape
    assert seq % (RES * BLK) == 0 and seq % SSD_TILE == 0 and seq % C_TILE == 0

    mod = _modulation(c, ada_w, ada_b)
    bias_first, bias_later = _dilated_biases()
    kpos = np.arange(2 * BLK) - BLK
    swa = _band_bias(np.arange(BLK), kpos, C_WINDOW)
    swa_first = np.where(kpos[None, :] >= 0, swa, NEG).astype(np.float32)
    bias_c = jnp.asarray(np.stack([swa_first, swa]))
    expand_np = np.zeros((LANES, B_WIDTH), np.float32)
    for h in range(B_HEADS):
        expand_np[h, h * HEAD_DIM:(h + 1) * HEAD_DIM] = 1.0
    expand = jnp.asarray(expand_np).astype(BF16)
    tri = jnp.asarray(np.tril(np.ones((SSD_CHUNK, SSD_CHUNK), np.float32))).astype(BF16)
    perm_np = _residue_perm()
    perm = jnp.asarray(perm_np).astype(BF16)
    perm_t = jnp.asarray(perm_np.T).astype(BF16)
    pad16 = lambda v: jnp.pad(v.astype(F32), (0, DT_PAD - B_HEADS)).reshape(1, DT_PAD)

    w_in_t = jnp.swapaxes(w_in, 1, 2)
    w_in_tail = _pack_w_in_tail(w_in_t)
    w_out_packed = _pack_w_out(w_out)
    for i in range(DEPTH):
        qa, ka, va, ga, gzb, xc, dt_raw, qc, gc, kvc = _inproj(
            x, mod[i], pre_norm_w[i].reshape(1, D_MODEL), w_in_t, w_in_tail, i, perm,
            conv_w[i], conv_b[i].reshape(1, B_CONV_CH))

        ya = _dilated_attention(qa, ka, va, ga, bias_first, bias_later)

        yb = _ssd(xc, gzb, dt_raw, pad16(dt_bias[i]), pad16(a_log[i]),
                  jnp.repeat(d_skip[i].astype(F32), HEAD_DIM).reshape(1, B_WIDTH),
                  ssm_norm_w[i].reshape(1, B_WIDTH), expand, tri)

        sink_lanes = jnp.repeat((sinks[i].astype(F32) * LOG2E).reshape(2, C_HEADS // 2).T, HEAD_DIM, axis=1)
        yc = _sink_attention(qc, gc, kvc, sink_lanes.reshape(C_HEADS // 2, 1, LANES), bias_c)

        x = _outproj(ya, yb, yc, x, mod[i], post_norm_w[i].reshape(1, D_MODEL),
                     w_out_packed, i, perm_t)
    return x
```

```python
import functools
import math

import numpy as np
import jax
import jax.numpy as jnp
from jax import lax
from jax.experimental import pallas as pl
from jax.experimental.pallas import tpu as pltpu

F32 = jnp.float32
BF16 = jnp.bfloat16

D_MODEL = 1024
DEPTH = 4
HEAD_DIM = 64
LANES = 128
BF16_ROWS = 16
A_WIDTH = 512
B_WIDTH = 1024
B_HEADS = 16
B_GROUPS = 2
B_STATE = 128
B_CONV = 4
B_CONV_CH = B_WIDTH + 2 * B_GROUPS * B_STATE
SSD_CHUNK = 128
C_WIDTH = 512
C_HEADS = 8
C_KV_WIDTH = 128
C_WINDOW = 128
A_WINDOW = 128
NORM_EPS = 1e-6
RES = 16
BLK = 128
NEG = -1e30
LOG2E = math.log2(math.e)
DT_PAD = 128
IN_COLS_MAIN = 4 * A_WIDTH + B_WIDTH + B_CONV_CH
IN_COLS_TAIL = 2 * C_WIDTH + 2 * C_KV_WIDTH + DT_PAD
PERM_TILE = RES * BF16_ROWS
PERM_PER_TILE = 2
ROW_TILE = PERM_TILE * PERM_PER_TILE
SEG_COLS = 256
VMEM_LIMIT = 56 * 1024 * 1024
UNROLL = 16


def _silu(v):
    return v * jax.nn.sigmoid(v)


def _split_dot(x, w_bf16):
    hi = x.astype(BF16)
    lo = (x - hi.astype(F32)).astype(BF16)
    return (jnp.dot(hi, w_bf16, preferred_element_type=F32)
            + jnp.dot(lo, w_bf16, preferred_element_type=F32))


def _residue_perm():
    p = np.zeros((PERM_TILE, PERM_TILE), np.float32)
    for r in range(RES):
        for jj in range(BF16_ROWS):
            p[r * BF16_ROWS + jj, RES * jj + r] = 1.0
    return p


def _mod_kernel(c_ref, w_ref, b_ref, o_ref):
    ca = _silu(c_ref[...])
    o_ref[...] = jnp.dot(ca, w_ref[...], preferred_element_type=F32,
                         precision=lax.Precision.HIGHEST) + b_ref[...]


def _modulation(c, ada_w, ada_b):
    bsz = c.shape[0]
    ada_b4 = ada_b.reshape(DEPTH, 3, 1, D_MODEL)
    return pl.pallas_call(
        _mod_kernel,
        out_shape=jax.ShapeDtypeStruct((DEPTH, bsz, 3 * D_MODEL), F32),
        grid=(DEPTH, 3),
        in_specs=[
            pl.BlockSpec((bsz, D_MODEL), lambda i, k: (0, 0)),
            pl.BlockSpec((None, D_MODEL, D_MODEL), lambda i, k: (i, 0, k)),
            pl.BlockSpec((None, None, 1, D_MODEL), lambda i, k: (i, k, 0, 0)),
        ],
        out_specs=pl.BlockSpec((None, bsz, D_MODEL), lambda i, k: (i, 0, k)),
        compiler_params=pltpu.CompilerParams(dimension_semantics=("arbitrary", "arbitrary")),
        name="ada_mod",
    )(c, ada_w, ada_b4).reshape(DEPTH, bsz, 1, 3 * D_MODEL)


def _inproj_kernel(x_ref, mod_ref, pw_ref, w32_ref, wt_ref, perm_ref, cw_ref, cb_ref,
                   qa_ref, ka_ref, va_ref, ga_ref, gb_ref, xc_ref, dt_ref,
                   qc_ref, gc_ref, kvc_ref, xpad, h, h_res, w_ref):
    @pl.when((pl.program_id(0) == 0) & (pl.program_id(1) == 0))
    def _():
        for c0 in range(0, IN_COLS_MAIN, SEG_COLS):
            w_ref[:, c0:c0 + SEG_COLS] = w32_ref[c0:c0 + SEG_COLS, :].T.astype(BF16)

    @pl.when(pl.program_id(1) == 0)
    def _():
        xpad[...] = jnp.zeros_like(xpad)

    qscale = HEAD_DIM ** -0.5 * LOG2E
    off_zb = 4 * A_WIDTH
    off_xbc = off_zb + B_WIDTH
    off_qc = off_xbc + B_CONV_CH
    off_kv = off_qc + 2 * C_WIDTH
    off_dt = off_kv + 2 * C_KV_WIDTH
    shift, scale_mod = mod_ref[:, 0:D_MODEL], mod_ref[:, D_MODEL:2 * D_MODEL]
    sub = lax.broadcasted_iota(jnp.int32, (CONV_PAD, SEG_COLS), 0)
    scaled = lambda v: v * qscale
    plain = lambda v: v

    def segments(width):
        return [(s, min(SEG_COLS, width - s)) for s in range(0, width, SEG_COLS)]

    for u in range(PERM_PER_TILE):
        rows = slice(u * PERM_TILE, (u + 1) * PERM_TILE)
        x = x_ref[rows, :]
        ms = jnp.mean(x * x, axis=-1, keepdims=True)
        xn = x * lax.rsqrt(ms + NORM_EPS) * pw_ref[...]
        h[rows, :] = (xn * (1.0 + scale_mod) + shift).astype(BF16)
        h_res[rows, :] = jnp.dot(perm_ref[...], h[rows, :], preferred_element_type=F32).astype(BF16)

        def proj(lhs, c0, width, rows=rows):
            if c0 >= IN_COLS_MAIN:
                rhs = wt_ref[:, c0 - IN_COLS_MAIN:c0 - IN_COLS_MAIN + width]
            else:
                rhs = w_ref[:, c0:c0 + width]
            return jnp.dot(lhs[rows, :], rhs, preferred_element_type=F32)

        def project_to(ref, c0, width, post, rows=rows, proj=proj):
            def job(s, w):
                ref[rows, s:s + w] = post(proj(h, c0 + s, w)).astype(ref.dtype)
            return [functools.partial(job, s, w) for s, w in segments(width)]

        def project_grouped(ref, c0, post, u=u, proj=proj):
            def job(s, w):
                val = post(proj(h_res, c0 + s, w)).astype(BF16)
                for r in range(RES):
                    ref[r, u * BF16_ROWS:(u + 1) * BF16_ROWS, s:s + w] = val[r * BF16_ROWS:(r + 1) * BF16_ROWS]
            return [functools.partial(job, s, w) for s, w in segments(A_WIDTH)]

        def conv_job(s, w, rows=rows, proj=proj):
            cols = slice(s, s + w)
            res = proj(h, off_xbc + s, w)
            tail = xpad[:, cols]
            conv = cb_ref[:, cols] + cw_ref[B_CONV - 1:B_CONV, cols] * res
            for k in range(1, B_CONV):
                rolled = pltpu.roll(res, k, 0)
                head = jnp.where(sub < k, pltpu.roll(tail, k, 0), rolled[0:CONV_PAD])
                prev = jnp.concatenate([head, rolled[CONV_PAD:]], axis=0)
                conv = conv + cw_ref[B_CONV - 1 - k:B_CONV - k, cols] * prev
            xc_ref[rows, cols] = _silu(conv).astype(BF16)
            xpad[:, cols] = res[PERM_TILE - CONV_PAD:, :]

        heavy = [functools.partial(conv_job, s, w) for s, w in segments(B_CONV_CH)]
        medium = (project_grouped(ga_ref, 3 * A_WIDTH, _silu) + project_to(gb_ref, off_zb, B_WIDTH, _silu)
                  + project_to(gc_ref, off_qc + C_WIDTH, C_WIDTH, _silu))
        light = (project_grouped(ka_ref, A_WIDTH, plain) + project_grouped(va_ref, 2 * A_WIDTH, plain)
                 + project_to(kvc_ref, off_kv, 2 * C_KV_WIDTH, plain) + project_to(dt_ref, off_dt, DT_PAD, plain)
                 + project_grouped(qa_ref, 0, scaled) + project_to(qc_ref, off_qc, C_WIDTH, scaled))
        order = []
        for i, job in enumerate(heavy):
            order += [job, light[i]]
        rest = light[len(heavy):]
        for i, job in enumerate(medium):
            order.append(job)
            if i < len(rest):
                order.append(rest[i])
        for job in order:
            job()


CONV_PAD = 8


def _inproj(x, mod_i, pre_w, w_main, w_tail, layer, perm, conv_w, conv_b):
    bsz, seq, _ = x.shape
    jn = seq // RES
    nat_spec = lambda width: pl.BlockSpec((None, ROW_TILE, width), lambda b, i: (b, i, 0))
    grp_spec = pl.BlockSpec((None, RES, BF16_ROWS * PERM_PER_TILE, A_WIDTH), lambda b, i: (b, 0, i, 0))
    nat = lambda width, dt: jax.ShapeDtypeStruct((bsz, seq, width), dt)
    grp = jax.ShapeDtypeStruct((bsz, RES, jn, A_WIDTH), BF16)
    return pl.pallas_call(
        _inproj_kernel,
        out_shape=[grp, grp, grp, grp,
                   nat(B_WIDTH, BF16), nat(B_CONV_CH, BF16), nat(DT_PAD, F32),
                   nat(C_WIDTH, BF16), nat(C_WIDTH, BF16), nat(2 * C_KV_WIDTH, BF16)],
        grid=(bsz, seq // ROW_TILE),
        in_specs=[
            nat_spec(D_MODEL),
            pl.BlockSpec((None, 1, 3 * D_MODEL), lambda b, i: (b, 0, 0)),
            pl.BlockSpec((1, D_MODEL), lambda b, i: (0, 0)),
            pl.BlockSpec((None, IN_COLS_MAIN, D_MODEL), lambda b, i: (layer, 0, 0),
                         pipeline_mode=pl.Buffered(1)),
            pl.BlockSpec((None, D_MODEL, IN_COLS_TAIL), lambda b, i: (layer, 0, 0),
                         pipeline_mode=pl.Buffered(1)),
            pl.BlockSpec((PERM_TILE, PERM_TILE), lambda b, i: (0, 0)),
            pl.BlockSpec(conv_w.shape, lambda b, i: (0, 0)),
            pl.BlockSpec(conv_b.shape, lambda b, i: (0, 0)),
        ],
        out_specs=[grp_spec, grp_spec, grp_spec, grp_spec,
                   nat_spec(B_WIDTH), nat_spec(B_CONV_CH), nat_spec(DT_PAD),
                   nat_spec(C_WIDTH), nat_spec(C_WIDTH), nat_spec(2 * C_KV_WIDTH)],
        scratch_shapes=[pltpu.VMEM((CONV_PAD, B_CONV_CH), F32),
                        pltpu.VMEM((ROW_TILE, D_MODEL), BF16), pltpu.VMEM((ROW_TILE, D_MODEL), BF16),
                        pltpu.VMEM((D_MODEL, IN_COLS_MAIN), BF16)],
        compiler_params=pltpu.CompilerParams(dimension_semantics=("arbitrary", "arbitrary"),
                                             vmem_limit_bytes=VMEM_LIMIT),
        name="in_proj",
    )(x, mod_i, pre_w, w_main, w_tail, perm, conv_w, conv_b)


def _head_masks():
    lane = lax.broadcasted_iota(jnp.int32, (BLK, LANES), 1)
    lo = lane < HEAD_DIM
    return lo, lo.astype(F32).astype(BF16), (~lo).astype(F32).astype(BF16)


def _attend_pairs(blocks, lo, mlo, mhi):
    scores = []
    for qb, kb, _, bias2 in blocks:
        qs = jnp.concatenate([qb * mlo, qb * mhi], axis=0)
        scores.append(lax.dot_general(qs, kb, (((1,), (1,)), ((), ())),
                                      preferred_element_type=F32) + bias2)
    outs = []
    for (_, _, vb, _), s in zip(blocks, scores):
        m = jnp.max(s, axis=-1, keepdims=True)
        p = jnp.exp2(s - m).astype(BF16)
        v_ext = jnp.concatenate([vb, jnp.ones_like(vb)], axis=1)
        pv = jnp.dot(p, v_ext, preferred_element_type=F32)
        outs.append((jnp.where(lo, m[:BLK], m[BLK:]),
                     jnp.where(lo, pv[:BLK, LANES:], pv[BLK:, LANES:]),
                     jnp.where(lo, pv[:BLK, :LANES], pv[BLK:, :LANES])))
    return outs


def _band_bias(qpos, kpos, window):
    dist = qpos[:, None] - kpos[None, :]
    ok = (dist >= 0) & (dist <= window)
    b = np.where(ok, 0.0, NEG).astype(np.float32)
    return np.concatenate([b, b], axis=0)


def _dilated_biases():
    first, later = [], []
    r, jj = np.meshgrid(np.arange(RES), np.arange(8), indexing="ij")
    qpos = (16 * jj + r).reshape(-1)
    r, jj = np.meshgrid(np.arange(RES), np.arange(16), indexing="ij")
    kpos = (16 * (jj - 8) + r).reshape(-1)
    first.append(_band_bias(qpos, qpos, A_WINDOW))
    later.append(_band_bias(qpos, kpos, A_WINDOW))
    rh, jj = np.meshgrid(np.arange(4), np.arange(32), indexing="ij")
    qpos = (4 * jj + rh).reshape(-1)
    rh, jj = np.meshgrid(np.arange(4), np.arange(64), indexing="ij")
    kpos = (4 * (jj - 32) + rh).reshape(-1)
    first.append(_band_bias(qpos, qpos, A_WINDOW))
    later.append(_band_bias(qpos, kpos, A_WINDOW))
    qpos = np.arange(BLK)
    kpos = np.arange(2 * BLK) - BLK
    first.append(_band_bias(qpos, qpos, A_WINDOW))
    later.append(_band_bias(qpos, kpos, A_WINDOW))
    return jnp.asarray(np.stack(first)), jnp.asarray(np.stack(later))


def _dilated_kernel(q_ref, k_ref, v_ref, g_ref, b1_ref, b2_ref, o_ref,
                    qf, kf, vf, m1, l1, a1, m2, l2, a2, m3, l3, a3):
    jn = q_ref.shape[1]
    lo, mlo, mhi = _head_masks()
    attend = functools.partial(_attend_pairs, lo=lo, mlo=mlo, mhi=mhi)

    def p3_blocks(r):
        blocks = [(q_ref[r, 0:BLK, :], k_ref[r, 0:BLK, :], v_ref[r, 0:BLK, :], b1_ref[2])]
        for jb in range(1, jn // BLK):
            s0 = (jb - 1) * BLK
            blocks.append((q_ref[r, s0 + BLK:s0 + 2 * BLK, :], k_ref[r, s0:s0 + 2 * BLK, :],
                           v_ref[r, s0:s0 + 2 * BLK, :], b2_ref[2]))
        return blocks

    def p3_store(r, outs):
        for jb, (mm, ll, acc) in enumerate(outs):
            m3[r, jb * BLK:(jb + 1) * BLK, :] = mm
            l3[r, jb * BLK:(jb + 1) * BLK, :] = ll
            a3[r, jb * BLK:(jb + 1) * BLK, :] = acc

    per3 = jn // BLK
    res3 = UNROLL // per3

    def p3_body(rr, carry):
        rs = [rr * res3 + d for d in range(res3)]
        outs = attend([blk for r in rs for blk in p3_blocks(r)])
        for d, r in enumerate(rs):
            p3_store(r, outs[d * per3:(d + 1) * per3])
        return carry
    lax.fori_loop(0, RES // res3, p3_body, 0)

    def p2_rows(ref, r4, start, size):
        return jnp.concatenate([ref[4 * rh + r4, pl.ds(start, size), :] for rh in range(4)], axis=0)

    def p2_store(r4, start, out):
        mm, ll, acc = out
        for rh in range(4):
            sl = slice(32 * rh, 32 * rh + 32)
            m2[4 * rh + r4, pl.ds(start, 32), :] = mm[sl]
            l2[4 * rh + r4, pl.ds(start, 32), :] = ll[sl]
            a2[4 * rh + r4, pl.ds(start, 32), :] = acc[sl]

    def p2_first(r4):
        return (p2_rows(q_ref, r4, 0, 32), p2_rows(k_ref, r4, 0, 32), p2_rows(v_ref, r4, 0, 32), b1_ref[1])

    def p2_later(r4, q0):
        k0 = q0 - 32 if isinstance(q0, int) else pl.multiple_of(q0 - 32, 32)
        return (p2_rows(q_ref, r4, q0, 32), p2_rows(k_ref, r4, k0, 64), p2_rows(v_ref, r4, k0, 64), b2_ref[1])

    steps2 = UNROLL // 4
    outs = attend([p2_first(r4) for r4 in range(4)]
                  + [p2_later(r4, 32 * b4) for b4 in range(1, steps2) for r4 in range(4)])
    for b4 in range(steps2):
        for r4 in range(4):
            p2_store(r4, 32 * b4, outs[4 * b4 + r4])

    def p2_body(i, carry):
        starts = [pl.multiple_of((i * steps2 + d) * 32, 32) for d in range(steps2)]
        outs = attend([p2_later(r4, q0) for q0 in starts for r4 in range(4)])
        for d, q0 in enumerate(starts):
            for r4 in range(4):
                p2_store(r4, q0, outs[4 * d + r4])
        return carry
    lax.fori_loop(1, jn // (32 * steps2), p2_body, 0)

    for r in range(RES):
        qf[r] = q_ref[r].astype(F32)
        kf[r] = k_ref[r].astype(F32)
        vf[r] = v_ref[r].astype(F32)

    def p1_rows(ref, start, size):
        return jnp.concatenate([ref[r, pl.ds(start, size), :] for r in range(RES)], axis=0).astype(BF16)

    def p1_store(start, out):
        mm, ll, acc = out
        for r in range(RES):
            sl = slice(8 * r, 8 * r + 8)
            m1[r, pl.ds(start, 8), :] = mm[sl]
            l1[r, pl.ds(start, 8), :] = ll[sl]
            a1[r, pl.ds(start, 8), :] = acc[sl]

    def p1_later(q0):
        k0 = q0 - 8 if isinstance(q0, int) else pl.multiple_of(q0 - 8, 8)
        return (p1_rows(qf, q0, 8), p1_rows(kf, k0, 16), p1_rows(vf, k0, 16), b2_ref[0])

    outs = attend([(p1_rows(qf, 0, 8), p1_rows(kf, 0, 8), p1_rows(vf, 0, 8), b1_ref[0])]
                  + [p1_later(8 * a) for a in range(1, UNROLL)])
    for a in range(UNROLL):
        p1_store(8 * a, outs[a])

    def p1_body(i, carry):
        starts = [pl.multiple_of((i * UNROLL + d) * 8, 8) for d in range(UNROLL)]
        outs = attend([p1_later(q0) for q0 in starts])
        for q0, out in zip(starts, outs):
            p1_store(q0, out)
        return carry
    lax.fori_loop(1, jn // (8 * UNROLL), p1_body, 0)

    def merge_body(r, carry):
        ma, mb, mc = m1[r], m2[r], m3[r]
        mx = jnp.maximum(jnp.maximum(ma, mb), mc)
        ea, eb, ec = jnp.exp2(ma - mx), jnp.exp2(mb - mx), jnp.exp2(mc - mx)
        num = ea * a1[r] + eb * a2[r] + ec * a3[r]
        den = ea * l1[r] + eb * l2[r] + ec * l3[r]
        o_ref[r] = (num / den * g_ref[r].astype(F32)).astype(BF16)
        return carry
    lax.fori_loop(0, RES, merge_body, 0)


def _dilated_attention(qa, ka, va, ga, bias_first, bias_later):
    bsz, _, jn, _ = qa.shape
    assert jn % (8 * UNROLL) == 0 and jn % BLK == 0
    spec = pl.BlockSpec((None, RES, jn, LANES), lambda b, hp: (b, 0, 0, hp))
    acc = pltpu.VMEM((RES, jn, LANES), F32)
    return pl.pallas_call(
        _dilated_kernel,
        out_shape=jax.ShapeDtypeStruct(qa.shape, BF16),
        grid=(bsz, A_WIDTH // LANES),
        in_specs=[spec, spec, spec, spec,
                  pl.BlockSpec(bias_first.shape, lambda b, hp: (0, 0, 0)),
                  pl.BlockSpec(bias_later.shape, lambda b, hp: (0, 0, 0))],
        out_specs=spec,
        scratch_shapes=[acc] * 12,
        compiler_params=pltpu.CompilerParams(dimension_semantics=("arbitrary", "arbitrary"),
                                             vmem_limit_bytes=VMEM_LIMIT),
        name="dilated_attn",
    )(qa, ka, va, ga, bias_first, bias_later)


C_TILE = 1024


def _sink_kernel(q_ref, g_ref, kv_ref, kvp_ref, sink_ref, bias_ref, o_ref):
    first_tile = pl.program_id(1) == 0
    lo, mlo, mhi = _head_masks()
    ngrp = C_WIDTH // LANES
    per_call = UNROLL // ngrp
    for u0 in range(0, C_TILE // BLK, per_call):
        blocks = []
        for u in range(u0, u0 + per_call):
            r0 = u * BLK
            cur = kv_ref[r0:r0 + BLK, :]
            if u == 0:
                prev = kvp_ref[...]
                bias = bias_ref[jnp.where(first_tile, 0, 1)]
            else:
                prev = kv_ref[r0 - BLK:r0, :]
                bias = bias_ref[1]
            kb = jnp.concatenate([prev[:, :LANES], cur[:, :LANES]], axis=0)
            vb = jnp.concatenate([prev[:, LANES:], cur[:, LANES:]], axis=0)
            for g in range(ngrp):
                blocks.append((q_ref[r0:r0 + BLK, g * LANES:(g + 1) * LANES], kb, vb, bias))
        outs = _attend_pairs(blocks, lo, mlo, mhi)
        for idx, (mm, ll, acc) in enumerate(outs):
            r0 = (u0 + idx // ngrp) * BLK
            cols = slice((idx % ngrp) * LANES, (idx % ngrp + 1) * LANES)
            sk = sink_ref[idx % ngrp]
            m2 = jnp.maximum(mm, sk)
            e = jnp.exp2(mm - m2)
            den = ll * e + jnp.exp2(sk - m2)
            gate = g_ref[r0:r0 + BLK, cols].astype(F32)
            o_ref[r0:r0 + BLK, cols] = (acc * e / den * gate).astype(BF16)


def _sink_attention(qc, gc, kvc, sink_lanes, bias):
    bsz, seq, _ = qc.shape
    per = C_TILE // BLK
    return pl.pallas_call(
        _sink_kernel,
        out_shape=jax.ShapeDtypeStruct(qc.shape, BF16),
        grid=(bsz, seq // C_TILE),
        in_specs=[
            pl.BlockSpec((None, C_TILE, C_WIDTH), lambda b, i: (b, i, 0)),
            pl.BlockSpec((None, C_TILE, C_WIDTH), lambda b, i: (b, i, 0)),
            pl.BlockSpec((None, C_TILE, 2 * C_KV_WIDTH), lambda b, i: (b, i, 0)),
            pl.BlockSpec((None, BLK, 2 * C_KV_WIDTH), lambda b, i: (b, jnp.maximum(i * per - 1, 0), 0)),
            pl.BlockSpec(sink_lanes.shape, lambda b, i: (0, 0, 0)),
            pl.BlockSpec(bias.shape, lambda b, i: (0, 0, 0)),
        ],
        out_specs=pl.BlockSpec((None, C_TILE, C_WIDTH), lambda b, i: (b, i, 0)),
        compiler_params=pltpu.CompilerParams(dimension_semantics=("arbitrary", "arbitrary"),
                                             vmem_limit_bytes=VMEM_LIMIT),
        name="sink_attn",
    )(qc, gc, kvc, kvc, sink_lanes, bias)


SSD_TILE = 1024
SSD_INTERLEAVE = 4


def _ssd_kernel(xc_ref, gz_ref, dt_ref, dtb_ref, alog_ref, dexp_ref, nw_ref,
                expand_ref, tri_ref, o_ref, state):
    q = SSD_CHUNK
    hg = B_HEADS // B_GROUPS
    gw = B_WIDTH // B_GROUPS

    @pl.when(pl.program_id(1) == 0)
    def _():
        state[...] = jnp.zeros_like(state)

    tri = tri_ref[...]
    causal = tri > 0
    expand = expand_ref[...]
    lo = lax.broadcasted_iota(jnp.int32, (q, LANES), 1) < HEAD_DIM
    a2 = -jnp.exp(alog_ref[...]) * LOG2E

    gcols = [slice(g * gw, (g + 1) * gw) for g in range(B_GROUPS)]
    st = [state[:, gc] for gc in gcols]

    for c_lo in range(0, SSD_TILE // q, SSD_INTERLEAVE):
        chunks = [dict(t0=c * q) for c in range(c_lo, c_lo + SSD_INTERLEAVE)]
        for ch in chunks:
            rows = slice(ch["t0"], ch["t0"] + q)
            ch["xs_b"] = xc_ref[rows, 0:B_WIDTH]
            dt = jax.nn.softplus(dt_ref[rows, :] + dtb_ref[...])
            da = dt * a2
            a_cum = _split_dot_left(tri, da)
            ch.update(dt=dt, a_cum=a_cum, a_cum_t=a_cum.T, dt_t=dt.T)
        for ch in chunks:
            a_cum = ch["a_cum"]
            dte = jnp.exp2(a_cum[q - 1:q, :] - a_cum)
            ch["ea_x"] = _split_dot(jnp.exp2(a_cum), expand)
            w_x = jnp.dot((ch["dt"] * dte).astype(BF16), expand, preferred_element_type=F32)
            ch["xw_b"] = (ch["xs_b"].astype(F32) * w_x).astype(BF16)
        for ch in chunks:
            rows = slice(ch["t0"], ch["t0"] + q)
            ch["cm_b"], ch["cbm"], ch["new"] = [], [], []
            for g in range(B_GROUPS):
                b0 = B_WIDTH + g * B_STATE
                c0 = B_WIDTH + (B_GROUPS + g) * B_STATE
                bm_b = xc_ref[rows, b0:b0 + B_STATE]
                cm_b = xc_ref[rows, c0:c0 + B_STATE]
                ch["cm_b"].append(cm_b)
                ch["cbm"].append(lax.dot_general(cm_b, bm_b, (((1,), (1,)), ((), ())),
                                                 preferred_element_type=F32))
                bm_t = bm_b.astype(F32).T.astype(BF16)
                ch["new"].append(jnp.dot(bm_t, ch["xw_b"][:, gcols[g]], preferred_element_type=F32))
        for ch in chunks:
            ch["y_off"] = []
            for g in range(B_GROUPS):
                ea_g = ch["ea_x"][:, gcols[g]]
                ch["y_off"].append(jnp.dot(ch["cm_b"][g], st[g].astype(BF16), preferred_element_type=F32) * ea_g)
                st[g] = st[g] * ea_g[q - 1:q, :] + ch["new"][g]
        for ch in chunks:
            y_parts = []
            for g in range(B_GROUPS):
                y_diag = []
                for pr in range(hg // 2):
                    h0 = g * hg + 2 * pr
                    mats = []
                    for h in (h0, h0 + 1):
                        seg = ch["a_cum"][:, h:h + 1] - ch["a_cum_t"][h:h + 1, :]
                        decay = jnp.exp2(jnp.where(causal, seg, NEG))
                        mats.append((ch["cbm"][g] * decay * ch["dt_t"][h:h + 1, :]).astype(BF16))
                    lhs = jnp.concatenate(mats, axis=0)
                    x0 = (h0 // 2) * LANES
                    yy = jnp.dot(lhs, ch["xs_b"][:, x0:x0 + LANES], preferred_element_type=F32)
                    y_diag.append(jnp.where(lo, yy[:q], yy[q:]))
                y_parts.append(jnp.concatenate(y_diag, axis=1) + ch["y_off"][g])
            ch["y"] = jnp.concatenate(y_parts, axis=1)
        for ch in chunks:
            rows = slice(ch["t0"], ch["t0"] + q)
            y = ch["y"] + dexp_ref[...] * ch["xs_b"].astype(F32)
            y = y * gz_ref[rows, :].astype(F32)
            outs = []
            for g in range(B_GROUPS):
                yg = y[:, gcols[g]]
                outs.append(yg * lax.rsqrt(jnp.mean(yg * yg, axis=-1, keepdims=True) + NORM_EPS))
            o_ref[rows, :] = (jnp.concatenate(outs, axis=1) * nw_ref[...]).astype(BF16)

    for g in range(B_GROUPS):
        state[:, gcols[g]] = st[g]


def _split_dot_left(w_bf16, x):
    hi = x.astype(BF16)
    lo = (x - hi.astype(F32)).astype(BF16)
    return (jnp.dot(w_bf16, hi, preferred_element_type=F32)
            + jnp.dot(w_bf16, lo, preferred_element_type=F32))


def _ssd(xc, gzb, dt_raw, dt_bias_row, a_log_row, d_exp, norm_w, expand, tri):
    bsz, seq, _ = xc.shape
    const = lambda a: pl.BlockSpec(a.shape, lambda b, i: (0,) * a.ndim)
    return pl.pallas_call(
        _ssd_kernel,
        out_shape=jax.ShapeDtypeStruct((bsz, seq, B_WIDTH), BF16),
        grid=(bsz, seq // SSD_TILE),
        in_specs=[
            pl.BlockSpec((None, SSD_TILE, B_CONV_CH), lambda b, i: (b, i, 0)),
            pl.BlockSpec((None, SSD_TILE, B_WIDTH), lambda b, i: (b, i, 0)),
            pl.BlockSpec((None, SSD_TILE, DT_PAD), lambda b, i: (b, i, 0)),
            const(dt_bias_row), const(a_log_row), const(d_exp), const(norm_w),
            const(expand), const(tri),
        ],
        out_specs=pl.BlockSpec((None, SSD_TILE, B_WIDTH), lambda b, i: (b, i, 0)),
        scratch_shapes=[pltpu.VMEM((B_STATE, B_WIDTH), F32)],
        compiler_params=pltpu.CompilerParams(dimension_semantics=("arbitrary", "arbitrary"),
                                             vmem_limit_bytes=VMEM_LIMIT),
        name="ssd",
    )(xc, gzb, dt_raw, dt_bias_row, a_log_row, d_exp, norm_w, expand, tri)


OUT_UNITS = 2
OUT_TILE = ROW_TILE * OUT_UNITS


def _outproj_kernel(ya_ref, yb_ref, yc_ref, x_ref, mod_ref, pw_ref, w_ref, permt_ref, o_ref):
    for unit in range(OUT_UNITS):
        rows = slice(unit * ROW_TILE, (unit + 1) * ROW_TILE)
        parts = []
        for u in range(unit * PERM_PER_TILE, (unit + 1) * PERM_PER_TILE):
            ya_res = jnp.concatenate([ya_ref[r, u * BF16_ROWS:(u + 1) * BF16_ROWS, :] for r in range(RES)], axis=0)
            parts.append(jnp.dot(permt_ref[...], ya_res, preferred_element_type=F32).astype(BF16))
        ya = jnp.concatenate(parts, axis=0)
        ycat = jnp.concatenate([ya, yb_ref[rows, :], yc_ref[rows, :]], axis=1)
        y = jnp.dot(ycat, w_ref[...], preferred_element_type=F32)
        ms = jnp.mean(y * y, axis=-1, keepdims=True)
        yn = y * lax.rsqrt(ms + NORM_EPS) * pw_ref[...]
        o_ref[rows, :] = x_ref[rows, :] + mod_ref[:, 2 * D_MODEL:] * yn


def _outproj(ya, yb, yc, x, mod_i, post_w, w_out_packed, layer, perm_t):
    bsz, seq, _ = x.shape
    nat_spec = lambda width: pl.BlockSpec((None, OUT_TILE, width), lambda b, i: (b, i, 0))
    return pl.pallas_call(
        _outproj_kernel,
        out_shape=jax.ShapeDtypeStruct(x.shape, F32),
        grid=(bsz, seq // OUT_TILE),
        in_specs=[
            pl.BlockSpec((None, RES, BF16_ROWS * PERM_PER_TILE * OUT_UNITS, A_WIDTH), lambda b, i: (b, 0, i, 0)),
            nat_spec(B_WIDTH), nat_spec(C_WIDTH), nat_spec(D_MODEL),
            pl.BlockSpec((None, 1, 3 * D_MODEL), lambda b, i: (b, 0, 0)),
            pl.BlockSpec((1, D_MODEL), lambda b, i: (0, 0)),
            pl.BlockSpec((None,) + w_out_packed.shape[1:], lambda b, i: (layer, 0, 0)),
            pl.BlockSpec((PERM_TILE, PERM_TILE), lambda b, i: (0, 0)),
        ],
        out_specs=nat_spec(D_MODEL),
        compiler_params=pltpu.CompilerParams(dimension_semantics=("arbitrary", "arbitrary"),
                                             vmem_limit_bytes=VMEM_LIMIT),
        name="out_proj",
    )(ya, yb, yc, x, mod_i, post_w, w_out_packed, perm_t)


def _c_head_order():
    ngrp = C_WIDTH // LANES
    return [g + ngrp * half for g in range(ngrp) for half in range(2)]


def _reorder_c_heads(w, axis):
    heads = [lax.slice_in_dim(w, h * HEAD_DIM, (h + 1) * HEAD_DIM, axis=axis) for h in _c_head_order()]
    return jnp.concatenate(heads, axis=axis)


def _pack_w_in_tail(w):
    t = lax.optimization_barrier(w[:, IN_COLS_MAIN:, :]).astype(BF16)
    qc = _reorder_c_heads(t[:, B_HEADS:B_HEADS + C_WIDTH], 1)
    zc = _reorder_c_heads(t[:, B_HEADS + C_WIDTH:B_HEADS + 2 * C_WIDTH], 1)
    kv = t[:, B_HEADS + 2 * C_WIDTH:B_HEADS + 2 * C_WIDTH + 2 * C_KV_WIDTH]
    dt = jnp.pad(t[:, :B_HEADS], ((0, 0), (0, DT_PAD - B_HEADS), (0, 0)))
    return jnp.swapaxes(jnp.concatenate([qc, zc, kv, dt], axis=1), 1, 2)


def _pack_w_out(w):
    c0 = A_WIDTH + B_WIDTH
    w = w.astype(BF16)
    return jnp.concatenate([w[:, :c0], _reorder_c_heads(w[:, c0:], 1)], axis=1)


def kernel(x, c, ada_w, ada_b, pre_norm_w, post_norm_w, w_in, conv_w, conv_b, dt_bias, a_log, d_skip,
           ssm_norm_w, sinks, w_out):
    bsz, seq, _ = x.shape
    assert seq % (RES * BLK) == 0 and seq % SSD_TILE == 0 and seq % C_TILE == 0

    mod = _modulation(c, ada_w, ada_b)
    bias_first, bias_later = _dilated_biases()
    kpos = np.arange(2 * BLK) - BLK
    swa = _band_bias(np.arange(BLK), kpos, C_WINDOW)
    swa_first = np.where(kpos[None, :] >= 0, swa, NEG).astype(np.float32)
    bias_c = jnp.asarray(np.stack([swa_first, swa]))
    expand_np = np.zeros((LANES, B_WIDTH), np.float32)
    for h in range(B_HEADS):
        expand_np[h, h * HEAD_DIM:(h + 1) * HEAD_DIM] = 1.0
    expand = jnp.asarray(expand_np).astype(BF16)
    tri = jnp.asarray(np.tril(np.ones((SSD_CHUNK, SSD_CHUNK), np.float32))).astype(BF16)
    perm_np = _residue_perm()
    perm = jnp.asarray(perm_np).astype(BF16)
    perm_t = jnp.asarray(perm_np.T).astype(BF16)
    pad16 = lambda v: jnp.pad(v.astype(F32), (0, DT_PAD - B_HEADS)).reshape(1, DT_PAD)

    w_in_t = jnp.swapaxes(w_in, 1, 2)
    w_in_tail = _pack_w_in_tail(w_in_t)
    w_out_packed = _pack_w_out(w_out)
    for i in range(DEPTH):
        qa, ka, va, ga, gzb, xc, dt_raw, qc, gc, kvc = _inproj(
            x, mod[i], pre_norm_w[i].reshape(1, D_MODEL), w_in_t, w_in_tail, i, perm,
            conv_w[i], conv_b[i].reshape(1, B_CONV_CH))

        ya = _dilated_attention(qa, ka, va, ga, bias_first, bias_later)

        yb = _ssd(xc, gzb, dt_raw, pad16(dt_bias[i]), pad16(a_log[i]),
                  jnp.repeat(d_skip[i].astype(F32), HEAD_DIM).reshape(1, B_WIDTH),
                  ssm_norm_w[i].reshape(1, B_WIDTH), expand, tri)

        sink_lanes = jnp.repeat((sinks[i].astype(F32) * LOG2E).reshape(2, C_HEADS // 2).T, HEAD_DIM, axis=1)
        yc = _sink_attention(qc, gc, kvc, sink_lanes.reshape(C_HEADS // 2, 1, LANES), bias_c)

        x = _outproj(ya, yb, yc, x, mod[i], post_norm_w[i].reshape(1, D_MODEL),
                     w_out_packed, i, perm_t)
    return x
```

```python
import functools
import math

import numpy as np
import jax
import jax.numpy as jnp
from jax import lax
from jax.experimental import pallas as pl
from jax.experimental.pallas import tpu as pltpu

F32 = jnp.float32
BF16 = jnp.bfloat16

D_MODEL = 1024
DEPTH = 4
HEAD_DIM = 64
LANES = 128
BF16_ROWS = 16
A_WIDTH = 512
B_WIDTH = 1024
B_HEADS = 16
B_GROUPS = 2
B_STATE = 128
B_CONV = 4
B_CONV_CH = B_WIDTH + 2 * B_GROUPS * B_STATE
SSD_CHUNK = 128
C_WIDTH = 512
C_HEADS = 8
C_KV_WIDTH = 128
C_WINDOW = 128
A_WINDOW = 128
NORM_EPS = 1e-6
RES = 16
BLK = 128
NEG = -1e30
LOG2E = math.log2(math.e)
DT_PAD = 128
IN_COLS_MAIN = 4 * A_WIDTH + B_WIDTH + B_CONV_CH
IN_COLS_TAIL = 2 * C_WIDTH + 2 * C_KV_WIDTH + DT_PAD
PERM_TILE = RES * BF16_ROWS
PERM_PER_TILE = 2
ROW_TILE = PERM_TILE * PERM_PER_TILE
SEG_COLS = 256
VMEM_LIMIT = 56 * 1024 * 1024
UNROLL = 16


def _silu(v):
    return v * jax.nn.sigmoid(v)


def _split_dot(x, w_bf16):
    hi = x.astype(BF16)
    lo = (x - hi.astype(F32)).astype(BF16)
    return (jnp.dot(hi, w_bf16, preferred_element_type=F32)
            + jnp.dot(lo, w_bf16, preferred_element_type=F32))


def _residue_perm():
    p = np.zeros((PERM_TILE, PERM_TILE), np.float32)
    for r in range(RES):
        for jj in range(BF16_ROWS):
            p[r * BF16_ROWS + jj, RES * jj + r] = 1.0
    return p


def _mod_kernel(c_ref, w_ref, b_ref, o_ref):
    ca = _silu(c_ref[...])
    o_ref[...] = jnp.dot(ca, w_ref[...], preferred_element_type=F32,
                         precision=lax.Precision.HIGHEST) + b_ref[...]


def _modulation(c, ada_w, ada_b):
    bsz = c.shape[0]
    ada_b4 = ada_b.reshape(DEPTH, 3, 1, D_MODEL)
    return pl.pallas_call(
        _mod_kernel,
        out_shape=jax.ShapeDtypeStruct((DEPTH, bsz, 3 * D_MODEL), F32),
        grid=(DEPTH, 3),
        in_specs=[
            pl.BlockSpec((bsz, D_MODEL), lambda i, k: (0, 0)),
            pl.BlockSpec((None, D_MODEL, D_MODEL), lambda i, k: (i, 0, k)),
            pl.BlockSpec((None, None, 1, D_MODEL), lambda i, k: (i, k, 0, 0)),
        ],
        out_specs=pl.BlockSpec((None, bsz, D_MODEL), lambda i, k: (i, 0, k)),
        compiler_params=pltpu.CompilerParams(dimension_semantics=("arbitrary", "arbitrary")),
        name="ada_mod",
    )(c, ada_w, ada_b4).reshape(DEPTH, bsz, 1, 3 * D_MODEL)


def _inproj_kernel(x_ref, mod_ref, pw_ref, w32_ref, wtt_ref, perm_ref, cw_ref, cb_ref,
                   qa_ref, ka_ref, va_ref, ga_ref, gb_ref, xc_ref, dt_ref,
                   qc_ref, gc_ref, kvc_ref, xpad, h, h_res, w_ref, wt_ref):
    @pl.when((pl.program_id(0) == 0) & (pl.program_id(1) == 0))
    def _():
        for c0 in range(0, IN_COLS_MAIN, SEG_COLS):
            w_ref[:, c0:c0 + SEG_COLS] = w32_ref[c0:c0 + SEG_COLS, :].T.astype(BF16)
        for c0 in range(0, IN_COLS_TAIL, LANES):
            wt_ref[:, c0:c0 + LANES] = wtt_ref[c0:c0 + LANES, :].astype(F32).T.astype(BF16)

    @pl.when(pl.program_id(1) == 0)
    def _():
        xpad[...] = jnp.zeros_like(xpad)

    qscale = HEAD_DIM ** -0.5 * LOG2E
    off_zb = 4 * A_WIDTH
    off_xbc = off_zb + B_WIDTH
    off_qc = off_xbc + B_CONV_CH
    off_kv = off_qc + 2 * C_WIDTH
    off_dt = off_kv + 2 * C_KV_WIDTH
    shift, scale_mod = mod_ref[:, 0:D_MODEL], mod_ref[:, D_MODEL:2 * D_MODEL]
    sub = lax.broadcasted_iota(jnp.int32, (CONV_PAD, SEG_COLS), 0)
    scaled = lambda v: v * qscale
    plain = lambda v: v

    def segments(width):
        return [(s, min(SEG_COLS, width - s)) for s in range(0, width, SEG_COLS)]

    for u in range(PERM_PER_TILE):
        rows = slice(u * PERM_TILE, (u + 1) * PERM_TILE)
        x = x_ref[rows, :]
        ms = jnp.mean(x * x, axis=-1, keepdims=True)
        xn = x * lax.rsqrt(ms + NORM_EPS) * pw_ref[...]
        h[rows, :] = (xn * (1.0 + scale_mod) + shift).astype(BF16)
        h_res[rows, :] = jnp.dot(perm_ref[...], h[rows, :], preferred_element_type=F32).astype(BF16)

        def proj(lhs, c0, width, rows=rows):
            if c0 >= IN_COLS_MAIN:
                rhs = wt_ref[:, c0 - IN_COLS_MAIN:c0 - IN_COLS_MAIN + width]
            else:
                rhs = w_ref[:, c0:c0 + width]
            return jnp.dot(lhs[rows, :], rhs, preferred_element_type=F32)

        def project_to(ref, c0, width, post, rows=rows, proj=proj):
            def job(s, w):
                ref[rows, s:s + w] = post(proj(h, c0 + s, w)).astype(ref.dtype)
            return [functools.partial(job, s, w) for s, w in segments(width)]

        def project_grouped(ref, c0, post, u=u, proj=proj):
            def job(s, w):
                val = post(proj(h_res, c0 + s, w)).astype(BF16)
                for r in range(RES):
                    ref[r, u * BF16_ROWS:(u + 1) * BF16_ROWS, s:s + w] = val[r * BF16_ROWS:(r + 1) * BF16_ROWS]
            return [functools.partial(job, s, w) for s, w in segments(A_WIDTH)]

        def conv_job(s, w, rows=rows, proj=proj):
            cols = slice(s, s + w)
            res = proj(h, off_xbc + s, w)
            tail = xpad[:, cols]
            conv = cb_ref[:, cols] + cw_ref[B_CONV - 1:B_CONV, cols] * res
            for k in range(1, B_CONV):
                rolled = pltpu.roll(res, k, 0)
                head = jnp.where(sub < k, pltpu.roll(tail, k, 0), rolled[0:CONV_PAD])
                prev = jnp.concatenate([head, rolled[CONV_PAD:]], axis=0)
                conv = conv + cw_ref[B_CONV - 1 - k:B_CONV - k, cols] * prev
            xc_ref[rows, cols] = _silu(conv).astype(BF16)
            xpad[:, cols] = res[PERM_TILE - CONV_PAD:, :]

        heavy = [functools.partial(conv_job, s, w) for s, w in segments(B_CONV_CH)]
        medium = (project_grouped(ga_ref, 3 * A_WIDTH, _silu) + project_to(gb_ref, off_zb, B_WIDTH, _silu)
                  + project_to(gc_ref, off_qc + C_WIDTH, C_WIDTH, _silu))
        light = (project_grouped(ka_ref, A_WIDTH, plain) + project_grouped(va_ref, 2 * A_WIDTH, plain)
                 + project_to(kvc_ref, off_kv, 2 * C_KV_WIDTH, plain) + project_to(dt_ref, off_dt, DT_PAD, plain)
                 + project_grouped(qa_ref, 0, scaled) + project_to(qc_ref, off_qc, C_WIDTH, scaled))
        order = []
        for i, job in enumerate(heavy):
            order += [job, light[i]]
        rest = light[len(heavy):]
        for i, job in enumerate(medium):
            order.append(job)
            if i < len(rest):
                order.append(rest[i])
        for job in order:
            job()


CONV_PAD = 8


def _inproj(x, mod_i, pre_w, w_main, w_tail, layer, perm, conv_w, conv_b):
    bsz, seq, _ = x.shape
    jn = seq // RES
    nat_spec = lambda width: pl.BlockSpec((None, ROW_TILE, width), lambda b, i: (b, i, 0))
    grp_spec = pl.BlockSpec((None, RES, BF16_ROWS * PERM_PER_TILE, A_WIDTH), lambda b, i: (b, 0, i, 0))
    nat = lambda width, dt: jax.ShapeDtypeStruct((bsz, seq, width), dt)
    grp = jax.ShapeDtypeStruct((bsz, RES, jn, A_WIDTH), BF16)
    return pl.pallas_call(
        _inproj_kernel,
        out_shape=[grp, grp, grp, grp,
                   nat(B_WIDTH, BF16), nat(B_CONV_CH, BF16), nat(DT_PAD, F32),
                   nat(C_WIDTH, BF16), nat(C_WIDTH, BF16), nat(2 * C_KV_WIDTH, BF16)],
        grid=(bsz, seq // ROW_TILE),
        in_specs=[
            nat_spec(D_MODEL),
            pl.BlockSpec((None, 1, 3 * D_MODEL), lambda b, i: (b, 0, 0)),
            pl.BlockSpec((1, D_MODEL), lambda b, i: (0, 0)),
            pl.BlockSpec((None, IN_COLS_MAIN, D_MODEL), lambda b, i: (layer, 0, 0),
                         pipeline_mode=pl.Buffered(1)),
            pl.BlockSpec((None, IN_COLS_TAIL, D_MODEL), lambda b, i: (layer, 0, 0),
                         pipeline_mode=pl.Buffered(1)),
            pl.BlockSpec((PERM_TILE, PERM_TILE), lambda b, i: (0, 0)),
            pl.BlockSpec(conv_w.shape, lambda b, i: (0, 0)),
            pl.BlockSpec(conv_b.shape, lambda b, i: (0, 0)),
        ],
        out_specs=[grp_spec, grp_spec, grp_spec, grp_spec,
                   nat_spec(B_WIDTH), nat_spec(B_CONV_CH), nat_spec(DT_PAD),
                   nat_spec(C_WIDTH), nat_spec(C_WIDTH), nat_spec(2 * C_KV_WIDTH)],
        scratch_shapes=[pltpu.VMEM((CONV_PAD, B_CONV_CH), F32),
                        pltpu.VMEM((ROW_TILE, D_MODEL), BF16), pltpu.VMEM((ROW_TILE, D_MODEL), BF16),
                        pltpu.VMEM((D_MODEL, IN_COLS_MAIN), BF16), pltpu.VMEM((D_MODEL, IN_COLS_TAIL), BF16)],
        compiler_params=pltpu.CompilerParams(dimension_semantics=("arbitrary", "arbitrary"),
                                             vmem_limit_bytes=VMEM_LIMIT),
        name="in_proj",
    )(x, mod_i, pre_w, w_main, w_tail, perm, conv_w, conv_b)


def _head_masks():
    lane = lax.broadcasted_iota(jnp.int32, (BLK, LANES), 1)
    lo = lane < HEAD_DIM
    return lo, lo.astype(F32).astype(BF16), (~lo).astype(F32).astype(BF16)


def _attend_pairs(blocks, lo, mlo, mhi):
    scores = []
    for qb, kb, _, bias2 in blocks:
        qs = jnp.concatenate([qb * mlo, qb * mhi], axis=0)
        scores.append(lax.dot_general(qs, kb, (((1,), (1,)), ((), ())),
                                      preferred_element_type=F32) + bias2)
    outs = []
    for (_, _, vb, _), s in zip(blocks, scores):
        m = jnp.max(s, axis=-1, keepdims=True)
        p = jnp.exp2(s - m).astype(BF16)
        v_ext = jnp.concatenate([vb, jnp.ones_like(vb)], axis=1)
        pv = jnp.dot(p, v_ext, preferred_element_type=F32)
        outs.append((jnp.where(lo, m[:BLK], m[BLK:]),
                     jnp.where(lo, pv[:BLK, LANES:], pv[BLK:, LANES:]),
                     jnp.where(lo, pv[:BLK, :LANES], pv[BLK:, :LANES])))
    return outs


def _band_bias(qpos, kpos, window):
    dist = qpos[:, None] - kpos[None, :]
    ok = (dist >= 0) & (dist <= window)
    b = np.where(ok, 0.0, NEG).astype(np.float32)
    return np.concatenate([b, b], axis=0)


def _dilated_biases():
    first, later = [], []
    r, jj = np.meshgrid(np.arange(RES), np.arange(8), indexing="ij")
    qpos = (16 * jj + r).reshape(-1)
    r, jj = np.meshgrid(np.arange(RES), np.arange(16), indexing="ij")
    kpos = (16 * (jj - 8) + r).reshape(-1)
    first.append(_band_bias(qpos, qpos, A_WINDOW))
    later.append(_band_bias(qpos, kpos, A_WINDOW))
    rh, jj = np.meshgrid(np.arange(4), np.arange(32), indexing="ij")
    qpos = (4 * jj + rh).reshape(-1)
    rh, jj = np.meshgrid(np.arange(4), np.arange(64), indexing="ij")
    kpos = (4 * (jj - 32) + rh).reshape(-1)
    first.append(_band_bias(qpos, qpos, A_WINDOW))
    later.append(_band_bias(qpos, kpos, A_WINDOW))
    qpos = np.arange(BLK)
    kpos = np.arange(2 * BLK) - BLK
    first.append(_band_bias(qpos, qpos, A_WINDOW))
    later.append(_band_bias(qpos, kpos, A_WINDOW))
    return jnp.asarray(np.stack(first)), jnp.asarray(np.stack(later))


def _dilated_kernel(q_ref, k_ref, v_ref, g_ref, b1_ref, b2_ref, o_ref,
                    qf, kf, vf, m1, l1, a1, m2, l2, a2, m3, l3, a3):
    jn = q_ref.shape[1]
    lo, mlo, mhi = _head_masks()
    attend = functools.partial(_attend_pairs, lo=lo, mlo=mlo, mhi=mhi)

    def p3_blocks(r):
        blocks = [(q_ref[r, 0:BLK, :], k_ref[r, 0:BLK, :], v_ref[r, 0:BLK, :], b1_ref[2])]
        for jb in range(1, jn // BLK):
            s0 = (jb - 1) * BLK
            blocks.append((q_ref[r, s0 + BLK:s0 + 2 * BLK, :], k_ref[r, s0:s0 + 2 * BLK, :],
                           v_ref[r, s0:s0 + 2 * BLK, :], b2_ref[2]))
        return blocks

    def p3_store(r, outs):
        for jb, (mm, ll, acc) in enumerate(outs):
            m3[r, jb * BLK:(jb + 1) * BLK, :] = mm
            l3[r, jb * BLK:(jb + 1) * BLK, :] = ll
            a3[r, jb * BLK:(jb + 1) * BLK, :] = acc

    per3 = jn // BLK
    res3 = UNROLL // per3

    def p3_body(rr, carry):
        rs = [rr * res3 + d for d in range(res3)]
        outs = attend([blk for r in rs for blk in p3_blocks(r)])
        for d, r in enumerate(rs):
            p3_store(r, outs[d * per3:(d + 1) * per3])
        return carry
    lax.fori_loop(0, RES // res3, p3_body, 0)

    def p2_rows(ref, r4, start, size):
        return jnp.concatenate([ref[4 * rh + r4, pl.ds(start, size), :] for rh in range(4)], axis=0)

    def p2_store(r4, start, out):
        mm, ll, acc = out
        for rh in range(4):
            sl = slice(32 * rh, 32 * rh + 32)
            m2[4 * rh + r4, pl.ds(start, 32), :] = mm[sl]
            l2[4 * rh + r4, pl.ds(start, 32), :] = ll[sl]
            a2[4 * rh + r4, pl.ds(start, 32), :] = acc[sl]

    def p2_first(r4):
        return (p2_rows(q_ref, r4, 0, 32), p2_rows(k_ref, r4, 0, 32), p2_rows(v_ref, r4, 0, 32), b1_ref[1])

    def p2_later(r4, q0):
        k0 = q0 - 32 if isinstance(q0, int) else pl.multiple_of(q0 - 32, 32)
        return (p2_rows(q_ref, r4, q0, 32), p2_rows(k_ref, r4, k0, 64), p2_rows(v_ref, r4, k0, 64), b2_ref[1])

    steps2 = UNROLL // 4
    outs = attend([p2_first(r4) for r4 in range(4)]
                  + [p2_later(r4, 32 * b4) for b4 in range(1, steps2) for r4 in range(4)])
    for b4 in range(steps2):
        for r4 in range(4):
            p2_store(r4, 32 * b4, outs[4 * b4 + r4])

    def p2_body(i, carry):
        starts = [pl.multiple_of((i * steps2 + d) * 32, 32) for d in range(steps2)]
        outs = attend([p2_later(r4, q0) for q0 in starts for r4 in range(4)])
        for d, q0 in enumerate(starts):
            for r4 in range(4):
                p2_store(r4, q0, outs[4 * d + r4])
        return carry
    lax.fori_loop(1, jn // (32 * steps2), p2_body, 0)

    for r in range(RES):
        qf[r] = q_ref[r].astype(F32)
        kf[r] = k_ref[r].astype(F32)
        vf[r] = v_ref[r].astype(F32)

    def p1_rows(ref, start, size):
        return jnp.concatenate([ref[r, pl.ds(start, size), :] for r in range(RES)], axis=0).astype(BF16)

    def p1_store(start, out):
        mm, ll, acc = out
        for r in range(RES):
            sl = slice(8 * r, 8 * r + 8)
            m1[r, pl.ds(start, 8), :] = mm[sl]
            l1[r, pl.ds(start, 8), :] = ll[sl]
            a1[r, pl.ds(start, 8), :] = acc[sl]

    def p1_later(q0):
        k0 = q0 - 8 if isinstance(q0, int) else pl.multiple_of(q0 - 8, 8)
        return (p1_rows(qf, q0, 8), p1_rows(kf, k0, 16), p1_rows(vf, k0, 16), b2_ref[0])

    outs = attend([(p1_rows(qf, 0, 8), p1_rows(kf, 0, 8), p1_rows(vf, 0, 8), b1_ref[0])]
                  + [p1_later(8 * a) for a in range(1, UNROLL)])
    for a in range(UNROLL):
        p1_store(8 * a, outs[a])

    def p1_body(i, carry):
        starts = [pl.multiple_of((i * UNROLL + d) * 8, 8) for d in range(UNROLL)]
        outs = attend([p1_later(q0) for q0 in starts])
        for q0, out in zip(starts, outs):
            p1_store(q0, out)
        return carry
    lax.fori_loop(1, jn // (8 * UNROLL), p1_body, 0)

    def merge_body(r, carry):
        ma, mb, mc = m1[r], m2[r], m3[r]
        mx = jnp.maximum(jnp.maximum(ma, mb), mc)
        ea, eb, ec = jnp.exp2(ma - mx), jnp.exp2(mb - mx), jnp.exp2(mc - mx)
        num = ea * a1[r] + eb * a2[r] + ec * a3[r]
        den = ea * l1[r] + eb * l2[r] + ec * l3[r]
        o_ref[r] = (num / den * g_ref[r].astype(F32)).astype(BF16)
        return carry
    lax.fori_loop(0, RES, merge_body, 0)


def _dilated_attention(qa, ka, va, ga, bias_first, bias_later):
    bsz, _, jn, _ = qa.shape
    assert jn % (8 * UNROLL) == 0 and jn % BLK == 0
    spec = pl.BlockSpec((None, RES, jn, LANES), lambda b, hp: (b, 0, 0, hp))
    acc = pltpu.VMEM((RES, jn, LANES), F32)
    return pl.pallas_call(
        _dilated_kernel,
        out_shape=jax.ShapeDtypeStruct(qa.shape, BF16),
        grid=(bsz, A_WIDTH // LANES),
        in_specs=[spec, spec, spec, spec,
                  pl.BlockSpec(bias_first.shape, lambda b, hp: (0, 0, 0)),
                  pl.BlockSpec(bias_later.shape, lambda b, hp: (0, 0, 0))],
        out_specs=spec,
        scratch_shapes=[acc] * 12,
        compiler_params=pltpu.CompilerParams(dimension_semantics=("arbitrary", "arbitrary"),
                                             vmem_limit_bytes=VMEM_LIMIT),
        name="dilated_attn",
    )(qa, ka, va, ga, bias_first, bias_later)


C_TILE = 1024


def _sink_kernel(q_ref, g_ref, kv_ref, kvp_ref, sink_ref, bias_ref, o_ref):
    first_tile = pl.program_id(1) == 0
    lo, mlo, mhi = _head_masks()
    ngrp = C_WIDTH // LANES
    per_call = UNROLL // ngrp
    for u0 in range(0, C_TILE // BLK, per_call):
        blocks = []
        for u in range(u0, u0 + per_call):
            r0 = u * BLK
            cur = kv_ref[r0:r0 + BLK, :]
            if u == 0:
                prev = kvp_ref[...]
                bias = bias_ref[jnp.where(first_tile, 0, 1)]
            else:
                prev = kv_ref[r0 - BLK:r0, :]
                bias = bias_ref[1]
            kb = jnp.concatenate([prev[:, :LANES], cur[:, :LANES]], axis=0)
            vb = jnp.concatenate([prev[:, LANES:], cur[:, LANES:]], axis=0)
            for g in range(ngrp):
                blocks.append((q_ref[r0:r0 + BLK, g * LANES:(g + 1) * LANES], kb, vb, bias))
        outs = _attend_pairs(blocks, lo, mlo, mhi)
        for idx, (mm, ll, acc) in enumerate(outs):
            r0 = (u0 + idx // ngrp) * BLK
            cols = slice((idx % ngrp) * LANES, (idx % ngrp + 1) * LANES)
            sk = sink_ref[idx % ngrp]
            m2 = jnp.maximum(mm, sk)
            e = jnp.exp2(mm - m2)
            den = ll * e + jnp.exp2(sk - m2)
            gate = g_ref[r0:r0 + BLK, cols].astype(F32)
            o_ref[r0:r0 + BLK, cols] = (acc * e / den * gate).astype(BF16)


def _sink_attention(qc, gc, kvc, sink_lanes, bias):
    bsz, seq, _ = qc.shape
    per = C_TILE // BLK
    return pl.pallas_call(
        _sink_kernel,
        out_shape=jax.ShapeDtypeStruct(qc.shape, BF16),
        grid=(bsz, seq // C_TILE),
        in_specs=[
            pl.BlockSpec((None, C_TILE, C_WIDTH), lambda b, i: (b, i, 0)),
            pl.BlockSpec((None, C_TILE, C_WIDTH), lambda b, i: (b, i, 0)),
            pl.BlockSpec((None, C_TILE, 2 * C_KV_WIDTH), lambda b, i: (b, i, 0)),
            pl.BlockSpec((None, BLK, 2 * C_KV_WIDTH), lambda b, i: (b, jnp.maximum(i * per - 1, 0), 0)),
            pl.BlockSpec(sink_lanes.shape, lambda b, i: (0, 0, 0)),
            pl.BlockSpec(bias.shape, lambda b, i: (0, 0, 0)),
        ],
        out_specs=pl.BlockSpec((None, C_TILE, C_WIDTH), lambda b, i: (b, i, 0)),
        compiler_params=pltpu.CompilerParams(dimension_semantics=("arbitrary", "arbitrary"),
                                             vmem_limit_bytes=VMEM_LIMIT),
        name="sink_attn",
    )(qc, gc, kvc, kvc, sink_lanes, bias)


SSD_TILE = 1024
SSD_INTERLEAVE = 4


def _ssd_kernel(xc_ref, gz_ref, dt_ref, dtb_ref, alog_ref, dexp_ref, nw_ref,
                expand_ref, tri_ref, o_ref, state):
    q = SSD_CHUNK
    hg = B_HEADS // B_GROUPS
    gw = B_WIDTH // B_GROUPS

    @pl.when(pl.program_id(1) == 0)
    def _():
        state[...] = jnp.zeros_like(state)

    tri = tri_ref[...]
    causal = tri > 0
    expand = expand_ref[...]
    lo = lax.broadcasted_iota(jnp.int32, (q, LANES), 1) < HEAD_DIM
    a2 = -jnp.exp(alog_ref[...]) * LOG2E

    gcols = [slice(g * gw, (g + 1) * gw) for g in range(B_GROUPS)]
    st = [state[:, gc] for gc in gcols]

    for c_lo in range(0, SSD_TILE // q, SSD_INTERLEAVE):
        chunks = [dict(t0=c * q) for c in range(c_lo, c_lo + SSD_INTERLEAVE)]
        for ch in chunks:
            rows = slice(ch["t0"], ch["t0"] + q)
            ch["xs_b"] = xc_ref[rows, 0:B_WIDTH]
            dt = jax.nn.softplus(dt_ref[rows, :] + dtb_ref[...])
            da = dt * a2
            a_cum = _split_dot_left(tri, da)
            ch.update(dt=dt, a_cum=a_cum, a_cum_t=a_cum.T, dt_t=dt.T)
        for ch in chunks:
            a_cum = ch["a_cum"]
            dte = jnp.exp2(a_cum[q - 1:q, :] - a_cum)
            ch["ea_x"] = _split_dot(jnp.exp2(a_cum), expand)
            w_x = jnp.dot((ch["dt"] * dte).astype(BF16), expand, preferred_element_type=F32)
            ch["xw_b"] = (ch["xs_b"].astype(F32) * w_x).astype(BF16)
        for ch in chunks:
            rows = slice(ch["t0"], ch["t0"] + q)
            ch["cm_b"], ch["cbm"], ch["new"] = [], [], []
            for g in range(B_GROUPS):
                b0 = B_WIDTH + g * B_STATE
                c0 = B_WIDTH + (B_GROUPS + g) * B_STATE
                bm_b = xc_ref[rows, b0:b0 + B_STATE]
                cm_b = xc_ref[rows, c0:c0 + B_STATE]
                ch["cm_b"].append(cm_b)
                ch["cbm"].append(lax.dot_general(cm_b, bm_b, (((1,), (1,)), ((), ())),
                                                 preferred_element_type=F32))
                bm_t = bm_b.astype(F32).T.astype(BF16)
                ch["new"].append(jnp.dot(bm_t, ch["xw_b"][:, gcols[g]], preferred_element_type=F32))
        for ch in chunks:
            ch["y_off"] = []
            for g in range(B_GROUPS):
                ea_g = ch["ea_x"][:, gcols[g]]
                ch["y_off"].append(jnp.dot(ch["cm_b"][g], st[g].astype(BF16), preferred_element_type=F32) * ea_g)
                st[g] = st[g] * ea_g[q - 1:q, :] + ch["new"][g]
        for ch in chunks:
            y_parts = []
            for g in range(B_GROUPS):
                y_diag = []
                for pr in range(hg // 2):
                    h0 = g * hg + 2 * pr
                    mats = []
                    for h in (h0, h0 + 1):
                        seg = ch["a_cum"][:, h:h + 1] - ch["a_cum_t"][h:h + 1, :]
                        decay = jnp.exp2(jnp.where(causal, seg, NEG))
                        mats.append((ch["cbm"][g] * decay * ch["dt_t"][h:h + 1, :]).astype(BF16))
                    lhs = jnp.concatenate(mats, axis=0)
                    x0 = (h0 // 2) * LANES
                    yy = jnp.dot(lhs, ch["xs_b"][:, x0:x0 + LANES], preferred_element_type=F32)
                    y_diag.append(jnp.where(lo, yy[:q], yy[q:]))
                y_parts.append(jnp.concatenate(y_diag, axis=1) + ch["y_off"][g])
            ch["y"] = jnp.concatenate(y_parts, axis=1)
        for ch in chunks:
            rows = slice(ch["t0"], ch["t0"] + q)
            y = ch["y"] + dexp_ref[...] * ch["xs_b"].astype(F32)
            y = y * gz_ref[rows, :].astype(F32)
            outs = []
            for g in range(B_GROUPS):
                yg = y[:, gcols[g]]
                outs.append(yg * lax.rsqrt(jnp.mean(yg * yg, axis=-1, keepdims=True) + NORM_EPS))
            o_ref[rows, :] = (jnp.concatenate(outs, axis=1) * nw_ref[...]).astype(BF16)

    for g in range(B_GROUPS):
        state[:, gcols[g]] = st[g]


def _split_dot_left(w_bf16, x):
    hi = x.astype(BF16)
    lo = (x - hi.astype(F32)).astype(BF16)
    return (jnp.dot(w_bf16, hi, preferred_element_type=F32)
            + jnp.dot(w_bf16, lo, preferred_element_type=F32))


def _ssd(xc, gzb, dt_raw, dt_bias_row, a_log_row, d_exp, norm_w, expand, tri):
    bsz, seq, _ = xc.shape
    const = lambda a: pl.BlockSpec(a.shape, lambda b, i: (0,) * a.ndim)
    return pl.pallas_call(
        _ssd_kernel,
        out_shape=jax.ShapeDtypeStruct((bsz, seq, B_WIDTH), BF16),
        grid=(bsz, seq // SSD_TILE),
        in_specs=[
            pl.BlockSpec((None, SSD_TILE, B_CONV_CH), lambda b, i: (b, i, 0)),
            pl.BlockSpec((None, SSD_TILE, B_WIDTH), lambda b, i: (b, i, 0)),
            pl.BlockSpec((None, SSD_TILE, DT_PAD), lambda b, i: (b, i, 0)),
            const(dt_bias_row), const(a_log_row), const(d_exp), const(norm_w),
            const(expand), const(tri),
        ],
        out_specs=pl.BlockSpec((None, SSD_TILE, B_WIDTH), lambda b, i: (b, i, 0)),
        scratch_shapes=[pltpu.VMEM((B_STATE, B_WIDTH), F32)],
        compiler_params=pltpu.CompilerParams(dimension_semantics=("arbitrary", "arbitrary"),
                                             vmem_limit_bytes=VMEM_LIMIT),
        name="ssd",
    )(xc, gzb, dt_raw, dt_bias_row, a_log_row, d_exp, norm_w, expand, tri)


OUT_UNITS = 2
OUT_TILE = ROW_TILE * OUT_UNITS


def _outproj_kernel(ya_ref, yb_ref, yc_ref, x_ref, mod_ref, pw_ref, w_ref, permt_ref, o_ref):
    for unit in range(OUT_UNITS):
        rows = slice(unit * ROW_TILE, (unit + 1) * ROW_TILE)
        parts = []
        for u in range(unit * PERM_PER_TILE, (unit + 1) * PERM_PER_TILE):
            ya_res = jnp.concatenate([ya_ref[r, u * BF16_ROWS:(u + 1) * BF16_ROWS, :] for r in range(RES)], axis=0)
            parts.append(jnp.dot(permt_ref[...], ya_res, preferred_element_type=F32).astype(BF16))
        ya = jnp.concatenate(parts, axis=0)
        ycat = jnp.concatenate([ya, yb_ref[rows, :], yc_ref[rows, :]], axis=1)
        y = jnp.dot(ycat, w_ref[...], preferred_element_type=F32)
        ms = jnp.mean(y * y, axis=-1, keepdims=True)
        yn = y * lax.rsqrt(ms + NORM_EPS) * pw_ref[...]
        o_ref[rows, :] = x_ref[rows, :] + mod_ref[:, 2 * D_MODEL:] * yn


def _outproj(ya, yb, yc, x, mod_i, post_w, w_out_packed, layer, perm_t):
    bsz, seq, _ = x.shape
    nat_spec = lambda width: pl.BlockSpec((None, OUT_TILE, width), lambda b, i: (b, i, 0))
    return pl.pallas_call(
        _outproj_kernel,
        out_shape=jax.ShapeDtypeStruct(x.shape, F32),
        grid=(bsz, seq // OUT_TILE),
        in_specs=[
            pl.BlockSpec((None, RES, BF16_ROWS * PERM_PER_TILE * OUT_UNITS, A_WIDTH), lambda b, i: (b, 0, i, 0)),
            nat_spec(B_WIDTH), nat_spec(C_WIDTH), nat_spec(D_MODEL),
            pl.BlockSpec((None, 1, 3 * D_MODEL), lambda b, i: (b, 0, 0)),
            pl.BlockSpec((1, D_MODEL), lambda b, i: (0, 0)),
            pl.BlockSpec((None,) + w_out_packed.shape[1:], lambda b, i: (layer, 0, 0)),
            pl.BlockSpec((PERM_TILE, PERM_TILE), lambda b, i: (0, 0)),
        ],
        out_specs=nat_spec(D_MODEL),
        compiler_params=pltpu.CompilerParams(dimension_semantics=("arbitrary", "arbitrary"),
                                             vmem_limit_bytes=VMEM_LIMIT),
        name="out_proj",
    )(ya, yb, yc, x, mod_i, post_w, w_out_packed, perm_t)


def _c_head_order():
    ngrp = C_WIDTH // LANES
    return [g + ngrp * half for g in range(ngrp) for half in range(2)]


def _reorder_c_heads(w, axis):
    heads = [lax.slice_in_dim(w, h * HEAD_DIM, (h + 1) * HEAD_DIM, axis=axis) for h in _c_head_order()]
    return jnp.concatenate(heads, axis=axis)


def _pack_w_in_tail(w):
    t = lax.optimization_barrier(w[:, IN_COLS_MAIN:, :]).astype(BF16)
    qc = _reorder_c_heads(t[:, B_HEADS:B_HEADS + C_WIDTH], 1)
    zc = _reorder_c_heads(t[:, B_HEADS + C_WIDTH:B_HEADS + 2 * C_WIDTH], 1)
    kv = t[:, B_HEADS + 2 * C_WIDTH:B_HEADS + 2 * C_WIDTH + 2 * C_KV_WIDTH]
    dt = jnp.pad(t[:, :B_HEADS], ((0, 0), (0, DT_PAD - B_HEADS), (0, 0)))
    return jnp.concatenate([qc, zc, kv, dt], axis=1)


def _pack_w_out(w):
    c0 = A_WIDTH + B_WIDTH
    w = w.astype(BF16)
    return jnp.concatenate([w[:, :c0], _reorder_c_heads(w[:, c0:], 1)], axis=1)


def kernel(x, c, ada_w, ada_b, pre_norm_w, post_norm_w, w_in, conv_w, conv_b, dt_bias, a_log, d_skip,
           ssm_norm_w, sinks, w_out):
    bsz, seq, _ = x.shape
    assert seq % (RES * BLK) == 0 and seq % SSD_TILE == 0 and seq % C_TILE == 0

    mod = _modulation(c, ada_w, ada_b)
    bias_first, bias_later = _dilated_biases()
    kpos = np.arange(2 * BLK) - BLK
    swa = _band_bias(np.arange(BLK), kpos, C_WINDOW)
    swa_first = np.where(kpos[None, :] >= 0, swa, NEG).astype(np.float32)
    bias_c = jnp.asarray(np.stack([swa_first, swa]))
    expand_np = np.zeros((LANES, B_WIDTH), np.float32)
    for h in range(B_HEADS):
        expand_np[h, h * HEAD_DIM:(h + 1) * HEAD_DIM] = 1.0
    expand = jnp.asarray(expand_np).astype(BF16)
    tri = jnp.asarray(np.tril(np.ones((SSD_CHUNK, SSD_CHUNK), np.float32))).astype(BF16)
    perm_np = _residue_perm()
    perm = jnp.asarray(perm_np).astype(BF16)
    perm_t = jnp.asarray(perm_np.T).astype(BF16)
    pad16 = lambda v: jnp.pad(v.astype(F32), (0, DT_PAD - B_HEADS)).reshape(1, DT_PAD)

    w_in_t = jnp.swapaxes(w_in, 1, 2)
    w_in_tail = _pack_w_in_tail(w_in_t)
    w_out_packed = _pack_w_out(w_out)
    for i in range(DEPTH):
        qa, ka, va, ga, gzb, xc, dt_raw, qc, gc, kvc = _inproj(
            x, mod[i], pre_norm_w[i].reshape(1, D_MODEL), w_in_t, w_in_tail, i, perm,
            conv_w[i], conv_b[i].reshape(1, B_CONV_CH))

        ya = _dilated_attention(qa, ka, va, ga, bias_first, bias_later)

        yb = _ssd(xc, gzb, dt_raw, pad16(dt_bias[i]), pad16(a_log[i]),
                  jnp.repeat(d_skip[i].astype(F32), HEAD_DIM).reshape(1, B_WIDTH),
                  ssm_norm_w[i].reshape(1, B_WIDTH), expand, tri)

        sink_lanes = jnp.repeat((sinks[i].astype(F32) * LOG2E).reshape(2, C_HEADS // 2).T, HEAD_DIM, axis=1)
        yc = _sink_attention(qc, gc, kvc, sink_lanes.reshape(C_HEADS // 2, 1, LANES), bias_c)

        x = _outproj(ya, yb, yc, x, mod[i], post_norm_w[i].reshape(1, D_MODEL),
                     w_out_packed, i, perm_t)
    return x
```
